```python
import math
import jax, jax.numpy as jnp
from jax import lax
import numpy as np

D_MODEL = 1024
BATCH = 8
SEQ = 4096
DEPTH = 4

POOL_WINDOWS = (2, 4, 8, 16)
N_POOL_GROUPS = 4
POOL_GROUP_DIM = 64
POOL_WIDTH = N_POOL_GROUPS * POOL_GROUP_DIM
DIFF_HEADS = 6
DIFF_HEAD_DIM = 64
DIFF_V_DIM = 2 * DIFF_HEAD_DIM
DIFF_QK = 2 * DIFF_HEADS * DIFF_HEAD_DIM
DIFF_WIDTH = DIFF_HEADS * DIFF_V_DIM
DIFF_QBLOCK = 128
CONV_CH = 256
CONV_WIDTH = 31
MOBA_HEADS = 6
MOBA_HEAD_DIM = 128
MOBA_WIDTH = MOBA_HEADS * MOBA_HEAD_DIM
MOBA_BLOCK = 256
MOBA_TOPK = 3
MOBA_QCHUNK = 16
MIX_WIDTH = POOL_WIDTH + DIFF_WIDTH
AB_IN = POOL_WIDTH + 2 * DIFF_QK + DIFF_WIDTH
CD_IN = 2 * CONV_CH + 3 * MOBA_WIDTH
D_FF = -(-8 * D_MODEL // (3 * 256)) * 256
N_AB = (DEPTH + 1) // 2
N_CD = DEPTH // 2
EPS = 1e-6
NEG = -1e30

kernel_name = "hybrid_pool_diffattn_conformer_moba_trunk"


def rms_norm(x, g):
    x32 = x.astype(jnp.float32)
    y = x32 * lax.rsqrt(jnp.mean(jnp.square(x32), axis=-1, keepdims=True) + EPS)
    return (y * g).astype(x.dtype)


def layer_norm(x, g, b):
    x32 = x.astype(jnp.float32)
    mu = jnp.mean(x32, axis=-1, keepdims=True)
    var = jnp.mean(jnp.square(x32 - mu), axis=-1, keepdims=True)
    return ((x32 - mu) * lax.rsqrt(var + EPS) * g + b).astype(x.dtype)


def multiscale_pool(u, w_groups, scale):
    B_, S, _ = u.shape
    u32 = u.astype(jnp.float32)
    cs = jnp.cumsum(u32, axis=1)
    cs = jnp.concatenate([jnp.zeros_like(cs[:, :1]), cs], axis=1)
    pos = jnp.arange(S)
    outs = []
    for g, w in enumerate(POOL_WINDOWS):
        sl = slice(g * POOL_GROUP_DIM, (g + 1) * POOL_GROUP_DIM)
        c = cs[:, :, sl]
        upper = c[:, 1:]
        lower = jnp.pad(c, ((0, 0), (w - 1, 0), (0, 0)))[:, :S]
        cnt = jnp.minimum(pos + 1, w).astype(jnp.float32)[None, :, None]
        outs.append((upper - lower) / cnt - u32[:, :, sl])
    d = jnp.stack(outs, axis=2).astype(u.dtype)
    y = jnp.einsum('bsgc,gcd->bsgd', d, w_groups)
    return y.reshape(B_, S, POOL_WIDTH) * scale


def diff_attention(q, k, v, lam_params, subln_g, lam_init):
    B_, S = q.shape[:2]
    lp = lam_params.astype(jnp.float32)
    lam = jnp.exp(jnp.sum(lp[0] * lp[1])) - jnp.exp(jnp.sum(lp[2] * lp[3])) + lam_init
    kh = k.transpose(0, 2, 1, 3)
    vh = v.transpose(0, 2, 1, 3)
    nblk = S // DIFF_QBLOCK
    qb = q.reshape(B_, nblk, DIFF_QBLOCK, 2 * DIFF_HEADS, DIFF_HEAD_DIM).transpose(1, 0, 3, 2, 4)
    key_pos = jnp.arange(S)
    scale = DIFF_HEAD_DIM ** -0.5

    def block(args):
        q_blk, bi = args
        s = jnp.einsum('bhqd,bhkd->bhqk', q_blk, kh).astype(jnp.float32) * scale
        q_pos = bi * DIFF_QBLOCK + jnp.arange(DIFF_QBLOCK)
        s = jnp.where(key_pos[None, :] <= q_pos[:, None], s, NEG)
        p = jax.nn.softmax(s, axis=-1).reshape(B_, DIFF_HEADS, 2, DIFF_QBLOCK, S)
        a = p[:, :, 0] - lam * p[:, :, 1]
        return jnp.einsum('bhqk,bhkd->bhqd', a.astype(vh.dtype), vh)

    o = lax.map(block, (qb, jnp.arange(nblk)))
    o = rms_norm(o, subln_g) * (1.0 - lam_init)
    return o.transpose(1, 0, 3, 2, 4).reshape(B_, S, DIFF_WIDTH)


def conformer_conv(u, conv_w, conv_b, ln_g, ln_b):
    h = u[..., :CONV_CH] * jax.nn.sigmoid(u[..., CONV_CH:])
    h = lax.conv_general_dilated(
        h, conv_w[:, None, :], window_strides=(1,), padding=[(CONV_WIDTH - 1, 0)],
        dimension_numbers=('NWC', 'WIO', 'NWC'), feature_group_count=CONV_CH) + conv_b
    return jax.nn.silu(layer_norm(h, ln_g, ln_b))


def moba_attention(q, k, v):
    B_, S = q.shape[:2]
    Sp = -(-S // MOBA_BLOCK) * MOBA_BLOCK
    pad = ((0, 0), (0, Sp - S), (0, 0), (0, 0))
    q, k, v = [jnp.pad(t, pad).transpose(0, 2, 1, 3) for t in (q, k, v)]
    nb = Sp // MOBA_BLOCK
    topk = min(MOBA_TOPK, nb)
    k_blocks = k.reshape(B_, MOBA_HEADS, nb, MOBA_BLOCK, MOBA_HEAD_DIM)
    v_blocks = v.reshape(B_, MOBA_HEADS, nb, MOBA_BLOCK, MOBA_HEAD_DIM)
    k_mean = jnp.mean(k_blocks.astype(jnp.float32), axis=3).astype(k.dtype)
    nch = Sp // MOBA_QCHUNK
    qc = q.reshape(B_, MOBA_HEADS, nch, MOBA_QCHUNK, MOBA_HEAD_DIM).transpose(2, 0, 1, 3, 4)
    b_idx = jnp.arange(B_)[:, None, None, None]
    h_idx = jnp.arange(MOBA_HEADS)[None, :, None, None]
    blk_ids = jnp.arange(nb)
    own_off = jnp.arange(MOBA_BLOCK)
    scale = MOBA_HEAD_DIM ** -0.5

    def chunk(args):
        q_c, ci = args
        start = ci * MOBA_QCHUNK
        i = start // MOBA_BLOCK
        q_pos = start + jnp.arange(MOBA_QCHUNK)
        gate = jnp.einsum('bhqd,bhnd->bhqn', q_c, k_mean).astype(jnp.float32)
        gate = jnp.where(blk_ids < i, gate, NEG)
        _, idx = lax.top_k(gate, topk)
        valid = idx < i
        k_sel = k_blocks[b_idx, h_idx, idx]
        v_sel = v_blocks[b_idx, h_idx, idx]
        s_sel = jnp.einsum('bhqd,bhqjkd->bhqjk', q_c, k_sel).astype(jnp.float32) * scale
        s_sel = jnp.where(valid[..., None], s_sel, NEG).reshape(B_, MOBA_HEADS, MOBA_QCHUNK, topk * MOBA_BLOCK)
        k_own = lax.dynamic_slice_in_dim(k, i * MOBA_BLOCK, MOBA_BLOCK, axis=2)
        v_own = lax.dynamic_slice_in_dim(v, i * MOBA_BLOCK, MOBA_BLOCK, axis=2)
        s_own = jnp.einsum('bhqd,bhkd->bhqk', q_c, k_own).astype(jnp.float32) * scale
        s_own = jnp.where(i * MOBA_BLOCK + own_off[None, :] <= q_pos[:, None], s_own, NEG)
        p = jax.nn.softmax(jnp.concatenate([s_sel, s_own], axis=-1), axis=-1)
        p_sel = p[..., :topk * MOBA_BLOCK].reshape(B_, MOBA_HEADS, MOBA_QCHUNK, topk, MOBA_BLOCK).astype(v.dtype)
        p_own = p[..., topk * MOBA_BLOCK:].astype(v.dtype)
        return (jnp.einsum('bhqjk,bhqjkd->bhqd', p_sel, v_sel)
                + jnp.einsum('bhqk,bhkd->bhqd', p_own, v_own))

    o = lax.map(chunk, (qc, jnp.arange(nch)))
    return o.transpose(1, 0, 3, 2, 4).reshape(B_, Sp, MOBA_WIDTH)[:, :S]


def setup_inputs(seed: int = 0) -> dict:
    key = jax.random.key(seed)
    ks = jax.random.split(key, 20)
    f32 = jnp.float32

    def nrm(k, shape, s):
        return jax.random.normal(k, shape, f32) * s

    return {
        "x": nrm(ks[0], (BATCH, SEQ, D_MODEL), 1.0),
        "norm_mix_g": 1.0 + nrm(ks[1], (DEPTH, D_MODEL), 0.02),
        "norm_ffn_g": 1.0 + nrm(ks[2], (DEPTH, D_MODEL), 0.02),
        "norm_final_g": 1.0 + nrm(ks[3], (D_MODEL,), 0.02),
        "ab_w_in": nrm(ks[4], (N_AB, D_MODEL, AB_IN), D_MODEL ** -0.5),
        "ab_w_out": nrm(ks[5], (N_AB, MIX_WIDTH, D_MODEL), MIX_WIDTH ** -0.5),
        "pool_w": nrm(ks[6], (N_AB, N_POOL_GROUPS, POOL_GROUP_DIM, POOL_GROUP_DIM), POOL_GROUP_DIM ** -0.5),
        "pool_scale": 1.0 + nrm(ks[7], (N_AB, POOL_WIDTH), 0.02),
        "diff_lambda": nrm(ks[8], (N_AB, 4, DIFF_HEAD_DIM), 0.1),
        "diff_subln_g": 1.0 + nrm(ks[9], (N_AB, DIFF_V_DIM), 0.02),
        "cd_w_in": nrm(ks[10], (N_CD, D_MODEL, CD_IN), D_MODEL ** -0.5),
        "cd_w_out": nrm(ks[11], (N_CD, MIX_WIDTH, D_MODEL), MIX_WIDTH ** -0.5),
        "conv_w": nrm(ks[12], (N_CD, CONV_WIDTH, CONV_CH), CONV_WIDTH ** -0.5),
        "conv_b": nrm(ks[13], (N_CD, CONV_CH), 0.02),
        "conv_ln_g": 1.0 + nrm(ks[14], (N_CD, CONV_CH), 0.02),
        "conv_ln_b": nrm(ks[15], (N_CD, CONV_CH), 0.02),
        "ffn_w_gate": nrm(ks[16], (DEPTH, D_MODEL, D_FF), D_MODEL ** -0.5),
        "ffn_w_up": nrm(ks[17], (DEPTH, D_MODEL, D_FF), D_MODEL ** -0.5),
        "ffn_w_down": nrm(ks[18], (DEPTH, D_FF, D_MODEL), D_FF ** -0.5),
    }


def reference(x, norm_mix_g, norm_ffn_g, norm_final_g, ab_w_in, ab_w_out, pool_w, pool_scale,
              diff_lambda, diff_subln_g, cd_w_in, cd_w_out, conv_w, conv_b, conv_ln_g, conv_ln_b,
              ffn_w_gate, ffn_w_up, ffn_w_down):
    B_, S, _ = x.shape
    h = x
    for layer in range(DEPTH):
        xn = rms_norm(h, norm_mix_g[layer])
        j = layer // 2
        if layer % 2 == 0:
            proj = xn @ ab_w_in[j]
            o = POOL_WIDTH
            u_pool = proj[..., :o]
            q = proj[..., o:o + DIFF_QK].reshape(B_, S, 2 * DIFF_HEADS, DIFF_HEAD_DIM)
            k = proj[..., o + DIFF_QK:o + 2 * DIFF_QK].reshape(B_, S, 2 * DIFF_HEADS, DIFF_HEAD_DIM)
            v = proj[..., o + 2 * DIFF_QK:].reshape(B_, S, DIFF_HEADS, DIFF_V_DIM)
            lam_init = 0.8 - 0.6 * math.exp(-0.3 * layer)
            y_a = multiscale_pool(u_pool, pool_w[j], pool_scale[j])
            y_b = diff_attention(q, k, v, diff_lambda[j], diff_subln_g[j], lam_init)
            mix = jnp.concatenate([y_a, y_b], axis=-1) @ ab_w_out[j]
        else:
            proj = xn @ cd_w_in[j]
            o = 2 * CONV_CH
            u_conv = proj[..., :o]
            q = proj[..., o:o + MOBA_WIDTH].reshape(B_, S, MOBA_HEADS, MOBA_HEAD_DIM)
            k = proj[..., o + MOBA_WIDTH:o + 2 * MOBA_WIDTH].reshape(B_, S, MOBA_HEADS, MOBA_HEAD_DIM)
            v = proj[..., o + 2 * MOBA_WIDTH:].reshape(B_, S, MOBA_HEADS, MOBA_HEAD_DIM)
            y_c = conformer_conv(u_conv, conv_w[j], conv_b[j], conv_ln_g[j], conv_ln_b[j])
            y_d = moba_attention(q, k, v)
            mix = jnp.concatenate([y_c, y_d], axis=-1) @ cd_w_out[j]
        h = h + mix
        hn = rms_norm(h, norm_ffn_g[layer])
        h = h + (jax.nn.silu(hn @ ffn_w_gate[layer]) * (hn @ ffn_w_up[layer])) @ ffn_w_down[layer]
    return rms_norm(h, norm_final_g)
```

```python
import functools
import math

import jax
import jax.numpy as jnp
from jax import lax
from jax.experimental import pallas as pl
from jax.experimental.pallas import tpu as pltpu

EPS = 1e-6
NEG = -1e30

POOL_WINDOWS = (2, 4, 8, 16)
POOL_GROUP_DIM = 64
POOL_WIDTH = 256
DIFF_HEADS = 6
DIFF_HEAD_DIM = 64
DIFF_QK = 768
CONV_CH = 256
CONV_WIDTH = 31
MOBA_HEADS = 6
MOBA_HEAD_DIM = 128
MOBA_WIDTH = 768
MOBA_BLOCK = 256
MOBA_TOPK = 3

LANES = 128
HEAD_COLS = 128
VMEM_LIMIT = 56 * 1024 * 1024

ROW_TILE = 512
DIFF_TILE = 256
CONV_CHUNK = 512
POOL_CHUNK = 512
FFN_CHUNKS = ((0, 768), (768, 1536), (1536, 2304), (2304, 2816))

bf16 = jnp.bfloat16
f32 = jnp.float32


def _cparams(sem):
    return pltpu.CompilerParams(dimension_semantics=sem, vmem_limit_bytes=VMEM_LIMIT)


def _resident(shape):
    nd = len(shape)
    return pl.BlockSpec(shape, lambda *_: (0,) * nd, pipeline_mode=pl.Buffered(1))


def _rms(x, g):
    ms = jnp.mean(x * x, axis=-1, keepdims=True)
    return x * lax.rsqrt(ms + EPS) * g


def _norm_proj_kernel(groups, x_ref, g_ref, w_ref, *out_refs):
    xn = _rms(x_ref[...], g_ref[...]).astype(bf16)
    for (a, b, scale, _), o_ref in zip(groups, out_refs):
        y = jnp.dot(xn, w_ref[:, a:b], preferred_element_type=f32)
        if scale != 1.0:
            y = y * scale
        o_ref[...] = y.astype(o_ref.dtype)


def _norm_proj(h, g, w, groups):
    n, d = h.shape
    return pl.pallas_call(
        functools.partial(_norm_proj_kernel, groups),
        grid=(n // ROW_TILE,),
        in_specs=[pl.BlockSpec((ROW_TILE, d), lambda i: (i, 0)),
                  _resident((1, d)),
                  _resident(w.shape)],
        out_specs=[pl.BlockSpec((ROW_TILE, b - a), lambda i: (i, 0)) for a, b, _, _ in groups],
        out_shape=[jax.ShapeDtypeStruct((n, b - a), dt) for a, b, _, dt in groups],
        compiler_params=_cparams(("parallel",)),
        name="norm_proj",
    )(h, g.reshape(1, d), w)


POOL_PAD = 16


def _pool_kernel(u_ref, w_ref, scale_ref, o_ref, pad_ref):
    s = u_ref.shape[0]
    pad_ref[0:POOL_PAD, :] = jnp.zeros((POOL_PAD, POOL_WIDTH), f32)
    pad_ref[POOL_PAD:, :] = u_ref[...]
    lane = lax.broadcasted_iota(jnp.int32, (POOL_CHUNK, POOL_WIDTH), 1)
    g0, g1, g2 = (lane < POOL_GROUP_DIM * (i + 1) for i in range(3))
    for c in range(s // POOL_CHUNK):
        base = c * POOL_CHUNK

        def win(k):
            return pad_ref[pl.ds(POOL_PAD + base - k, POOL_CHUNK), :]

        u = win(0)
        pos = base + lax.broadcasted_iota(jnp.int32, (POOL_CHUNK, POOL_WIDTH), 0)
        s2 = u + win(1)
        s4 = s2 + win(2) + win(3)
        s8 = s4 + win(4) + win(5) + win(6) + win(7)
        s16 = s8
        for k in range(8, 16):
            s16 = s16 + win(k)
        tot = jnp.where(g0, s2, jnp.where(g1, s4, jnp.where(g2, s8, s16)))
        wlen = jnp.where(g0, 2, jnp.where(g1, 4, jnp.where(g2, 8, 16)))
        cnt = jnp.minimum(pos + 1, wlen).astype(f32)
        d = (tot / cnt - u).astype(bf16)
        y = jnp.dot(d, w_ref[...], preferred_element_type=f32) * scale_ref[...]
        o_ref[pl.ds(base, POOL_CHUNK), :] = y.astype(o_ref.dtype)


def _pool(u, w_blockdiag, scale):
    b, s, c = u.shape
    return pl.pallas_call(
        _pool_kernel,
        grid=(b,),
        in_specs=[pl.BlockSpec((None, s, c), lambda i: (i, 0, 0)),
                  _resident((c, c)),
                  _resident((1, c))],
        out_specs=pl.BlockSpec((None, s, c), lambda i: (i, 0, 0)),
        out_shape=jax.ShapeDtypeStruct((b, s, c), bf16),
        scratch_shapes=[pltpu.VMEM((s + POOL_PAD, c), f32)],
        compiler_params=_cparams(("parallel",)),
        name="pool",
    )(u, w_blockdiag, scale.reshape(1, c))


def _scores(q, k):
    return lax.dot_general(q, k, (((1,), (1,)), ((), ())), preferred_element_type=f32)


def _flash_first(s, v, m_ref, l_ref, acc_ref):
    m = jnp.max(s, axis=-1, keepdims=True)
    p = jnp.exp(s - m)
    m_ref[...] = m
    l_ref[...] = jnp.sum(p, axis=-1, keepdims=True)
    acc_ref[...] = jnp.dot(p.astype(bf16), v, preferred_element_type=f32)


def _flash_step(s, v, bias, m_ref, l_ref, acc_ref):
    m_prev = m_ref[...]
    row_max = jnp.max(s, axis=-1, keepdims=True)
    if bias is None:
        m_new = jnp.maximum(m_prev, row_max)
        shift = m_new
    else:
        m_new = jnp.maximum(m_prev, row_max + bias)
        shift = m_new - bias
    alpha = jnp.exp(m_prev - m_new)
    p = jnp.exp(s - shift)
    l_ref[...] = alpha * l_ref[...] + jnp.sum(p, axis=-1, keepdims=True)
    acc_ref[...] = alpha * acc_ref[...] + jnp.dot(p.astype(bf16), v, preferred_element_type=f32)
    m_ref[...] = m_new


def _diff_kernel(lam_init, q_ref, k_ref, v_ref, lam_ref, g_ref, o_ref, qq_ref, m_ref, l_ref, acc_ref):
    t = DIFF_TILE
    qi = pl.program_id(2)
    q = q_ref[...]
    lane = lax.broadcasted_iota(jnp.int32, (t, HEAD_COLS), 1)
    zero = jnp.zeros_like(q)
    qq_ref[0:t, :] = jnp.where(lane < DIFF_HEAD_DIM, q, zero)
    qq_ref[t:, :] = jnp.where(lane >= DIFF_HEAD_DIM, q, zero)
    qq = qq_ref[...]

    start = pl.multiple_of(qi * t, t)
    s = _scores(qq, k_ref[pl.ds(start, t), :])
    row = lax.broadcasted_iota(jnp.int32, (2 * t, t), 0) & (t - 1)
    col = lax.broadcasted_iota(jnp.int32, (2 * t, t), 1)
    s = jnp.where(col <= row, s, NEG)
    _flash_first(s, v_ref[pl.ds(start, t), :], m_ref, l_ref, acc_ref)

    def body(j, carry):
        st = pl.multiple_of(j * t, t)
        _flash_step(_scores(qq, k_ref[pl.ds(st, t), :]), v_ref[pl.ds(st, t), :], None,
                    m_ref, l_ref, acc_ref)
        return carry

    lax.fori_loop(0, qi, body, 0)

    lp = lam_ref[...]
    lam = (jnp.exp(jnp.sum(lp[0:1] * lp[1:2], axis=-1, keepdims=True))
           - jnp.exp(jnp.sum(lp[2:3] * lp[3:4], axis=-1, keepdims=True)) + lam_init)
    o1 = acc_ref[0:t, :] / l_ref[0:t, :]
    o2 = acc_ref[t:, :] / l_ref[t:, :]
    o = o1 - lam * o2
    o_ref[...] = (_rms(o, g_ref[...]) * (1.0 - lam_init)).astype(o_ref.dtype)


def _diff_attention(q, k, v, lam_params, subln_g, lam_init):
    b, s, w = q.shape
    t = DIFF_TILE
    heads = w // HEAD_COLS
    tile = pl.BlockSpec((None, t, HEAD_COLS), lambda bi, h, qi: (bi, qi, h))
    full = pl.BlockSpec((None, s, HEAD_COLS), lambda bi, h, qi: (bi, 0, h))
    return pl.pallas_call(
        functools.partial(_diff_kernel, lam_init),
        grid=(b, heads, s // t),
        in_specs=[tile, full, full, _resident(lam_params.shape), _resident((1, HEAD_COLS))],
        out_specs=tile,
        out_shape=jax.ShapeDtypeStruct((b, s, w), bf16),
        scratch_shapes=[pltpu.VMEM((2 * t, HEAD_COLS), bf16),
                        pltpu.VMEM((2 * t, 1), f32),
                        pltpu.VMEM((2 * t, 1), f32),
                        pltpu.VMEM((2 * t, HEAD_COLS), f32)],
        compiler_params=_cparams(("parallel", "parallel", "arbitrary")),
        name="diff_attn",
    )(q, k, v, lam_params, subln_g.reshape(1, HEAD_COLS))


def _moba_kernel(q_ref, k_ref, v_ref, o_ref, kmean_ref, bias_ref, m_ref, l_ref, acc_ref):
    t = MOBA_BLOCK
    nblk = k_ref.shape[0] // t
    qi = pl.program_id(2)

    @pl.when(qi == 0)
    def _():
        kmean_ref[...] = jnp.zeros_like(kmean_ref)
        for n in range(nblk):
            kb = k_ref[n * t:(n + 1) * t, :].astype(f32)
            kmean_ref[n:n + 1, :] = jnp.mean(kb, axis=0, keepdims=True)

    q = q_ref[...]

    start = pl.multiple_of(qi * t, t)
    s = _scores(q, k_ref[pl.ds(start, t), :])
    row = lax.broadcasted_iota(jnp.int32, (t, t), 0)
    col = lax.broadcasted_iota(jnp.int32, (t, t), 1)
    s = jnp.where(col <= row, s, NEG)
    _flash_first(s, v_ref[pl.ds(start, t), :], m_ref, l_ref, acc_ref)

    @pl.when(qi > 0)
    def _():
        gate = lax.dot_general(q.astype(f32), kmean_ref[...], (((1,), (1,)), ((), ())),
                               preferred_element_type=f32, precision=lax.Precision.HIGHEST)
        lane_i = lax.broadcasted_iota(jnp.int32, (t, LANES), 1)
        valid = lane_i < qi
        lane = lane_i.astype(f32)
        g = jnp.where(valid, gate, NEG)
        sel = jnp.zeros((t, LANES), jnp.bool_)
        for _ in range(MOBA_TOPK):
            mx = jnp.max(g, axis=-1, keepdims=True)
            first = jnp.min(jnp.where(g == mx, lane, float(LANES)), axis=-1, keepdims=True)
            pick = lane == first
            sel = jnp.logical_or(sel, jnp.logical_and(pick, valid))
            g = jnp.where(pick, -jnp.inf, g)
        bias = jnp.where(sel, 0.0, NEG)
        for n in range(nblk - 1):
            @pl.when(n < qi)
            def _():
                bias_ref[n] = jnp.broadcast_to(bias[:, n:n + 1], (t, LANES))

        def body(j, carry):
            st = pl.multiple_of(j * t, t)
            _flash_step(_scores(q, k_ref[pl.ds(st, t), :]), v_ref[pl.ds(st, t), :],
                        bias_ref[j][:, 0:1], m_ref, l_ref, acc_ref)
            return carry

        lax.fori_loop(0, qi, body, 0)

    o_ref[...] = (acc_ref[...] / l_ref[...]).astype(o_ref.dtype)


def _moba_attention(q, k, v):
    b, s, w = q.shape
    t = MOBA_BLOCK
    assert s % t == 0 and s // t <= LANES
    heads = w // HEAD_COLS
    tile = pl.BlockSpec((None, t, HEAD_COLS), lambda bi, h, qi: (bi, qi, h))
    full = pl.BlockSpec((None, s, HEAD_COLS), lambda bi, h, qi: (bi, 0, h))
    return pl.pallas_call(
        _moba_kernel,
        grid=(b, heads, s // t),
        in_specs=[tile, full, full],
        out_specs=tile,
        out_shape=jax.ShapeDtypeStruct((b, s, w), bf16),
        scratch_shapes=[pltpu.VMEM((LANES, HEAD_COLS), f32),
                        pltpu.VMEM((s // t, t, LANES), f32),
                        pltpu.VMEM((t, 1), f32),
                        pltpu.VMEM((t, 1), f32),
                        pltpu.VMEM((t, HEAD_COLS), f32)],
        compiler_params=_cparams(("parallel", "parallel", "arbitrary")),
        name="moba_attn",
    )(q, k, v)


CONV_PAD = 32


def _conv_kernel(u_ref, w_ref, b_ref, g_ref, beta_ref, o_ref, pad_ref):
    s = u_ref.shape[0]
    pad_ref[0:CONV_PAD, :] = jnp.zeros((CONV_PAD, CONV_CH), f32)
    for c in range(s // CONV_CHUNK):
        rows = pl.ds(c * CONV_CHUNK, CONV_CHUNK)
        u = u_ref[rows, :]
        pad_ref[pl.ds(CONV_PAD + c * CONV_CHUNK, CONV_CHUNK), :] = (
            u[:, :CONV_CH] * jax.nn.sigmoid(u[:, CONV_CH:]))
    w = w_ref[...]
    for c in range(s // CONV_CHUNK):
        base = CONV_PAD + c * CONV_CHUNK - (CONV_WIDTH - 1)
        acc = jnp.zeros((CONV_CHUNK, CONV_CH), f32)
        for j in range(CONV_WIDTH):
            acc = acc + pad_ref[pl.ds(base + j, CONV_CHUNK), :] * w[j:j + 1, :]
        h = acc + b_ref[...]
        mu = jnp.mean(h, axis=-1, keepdims=True)
        var = jnp.mean(jnp.square(h - mu), axis=-1, keepdims=True)
        y = (h - mu) * lax.rsqrt(var + EPS) * g_ref[...] + beta_ref[...]
        o_ref[pl.ds(c * CONV_CHUNK, CONV_CHUNK), :] = (y * jax.nn.sigmoid(y)).astype(o_ref.dtype)


def _conformer_conv(u, conv_w, conv_b, ln_g, ln_b):
    b, s, c2 = u.shape
    c = c2 // 2
    vec = _resident((1, c))
    return pl.pallas_call(
        _conv_kernel,
        grid=(b,),
        in_specs=[pl.BlockSpec((None, s, c2), lambda i: (i, 0, 0)),
                  _resident(conv_w.shape), vec, vec, vec],
        out_specs=pl.BlockSpec((None, s, c), lambda i: (i, 0, 0)),
        out_shape=jax.ShapeDtypeStruct((b, s, c), bf16),
        scratch_shapes=[pltpu.VMEM((s + CONV_PAD, c), f32)],
        compiler_params=_cparams(("parallel",)),
        name="conformer_conv",
    )(u, conv_w, conv_b.reshape(1, c), ln_g.reshape(1, c), ln_b.reshape(1, c))


def _mix_ffn_kernel(final, h_ref, ya_ref, yb_ref, woa_ref, wob_ref, g_ref, wg_ref, wu_ref, wd_ref,
                    gf_ref, o_ref, act_ref):
    h1 = (h_ref[...]
          + jnp.dot(ya_ref[...], woa_ref[...], preferred_element_type=f32)
          + jnp.dot(yb_ref[...], wob_ref[...], preferred_element_type=f32))
    hn = _rms(h1, g_ref[...]).astype(bf16)
    for a, b in FFN_CHUNKS:
        gate = jnp.dot(hn, wg_ref[:, a:b], preferred_element_type=f32)
        up = jnp.dot(hn, wu_ref[:, a:b], preferred_element_type=f32)
        act_ref[:, a:b] = (gate * jax.nn.sigmoid(gate) * up).astype(bf16)
    h2 = h1 + jnp.dot(act_ref[...], wd_ref[...], preferred_element_type=f32)
    if final:
        h2 = _rms(h2, gf_ref[...])
    o_ref[...] = h2


def _mix_ffn(h, ya, yb, w_out, g, wg, wu, wd, g_final, final):
    n, d = h.shape
    ca, cb = ya.shape[1], yb.shape[1]
    dff = wg.shape[1]
    assert FFN_CHUNKS[-1][1] == dff
    woa, wob = w_out[:ca], w_out[ca:]
    row = lambda c: pl.BlockSpec((ROW_TILE, c), lambda i: (i, 0))
    return pl.pallas_call(
        functools.partial(_mix_ffn_kernel, final),
        grid=(n // ROW_TILE,),
        in_specs=[row(d), row(ca), row(cb), _resident(woa.shape), _resident(wob.shape),
                  _resident((1, d)), _resident(wg.shape), _resident(wu.shape), _resident(wd.shape),
                  _resident((1, d))],
        out_specs=row(d),
        out_shape=jax.ShapeDtypeStruct((n, d), f32),
        scratch_shapes=[pltpu.VMEM((ROW_TILE, dff), bf16)],
        compiler_params=_cparams(("parallel",)),
        name="mix_ffn",
    )(h, ya, yb, woa, wob, g.reshape(1, d), wg, wu, wd, g_final.reshape(1, d))


def _block_diag(w):
    g, c, _ = w.shape
    eye = jnp.eye(g, dtype=w.dtype)
    return (eye[:, None, :, None] * w[:, :, None, :]).reshape(g * c, g * c)


def kernel(x, norm_mix_g, norm_ffn_g, norm_final_g, ab_w_in, ab_w_out, pool_w, pool_scale, diff_lambda, diff_subln_g, cd_w_in, cd_w_out, conv_w, conv_b, conv_ln_g, conv_ln_b, ffn_w_gate, ffn_w_up, ffn_w_down):
    bsz, s, d = x.shape
    n = bsz * s
    depth = norm_mix_g.shape[0]
    assert n % ROW_TILE == 0 and s % DIFF_TILE == 0 and s % CONV_CHUNK == 0 and s % POOL_CHUNK == 0
    h = x.reshape(n, d)
    for layer in range(depth):
        j = layer // 2
        if layer % 2 == 0:
            o = POOL_WIDTH
            groups = ((0, o, 1.0, f32),
                      (o, o + DIFF_QK, DIFF_HEAD_DIM ** -0.5, bf16),
                      (o + DIFF_QK, o + 2 * DIFF_QK, 1.0, bf16),
                      (o + 2 * DIFF_QK, o + 3 * DIFF_QK, 1.0, bf16))
            u, q, k, v = _norm_proj(h, norm_mix_g[layer], ab_w_in[j].astype(bf16), groups)
            lam_init = 0.8 - 0.6 * math.exp(-0.3 * layer)
            ya = _pool(u.reshape(bsz, s, -1), _block_diag(pool_w[j]).astype(bf16), pool_scale[j])
            yb = _diff_attention(q.reshape(bsz, s, -1), k.reshape(bsz, s, -1), v.reshape(bsz, s, -1),
                                 diff_lambda[j], diff_subln_g[j], lam_init)
            w_out = ab_w_out[j]
        else:
            o = 2 * CONV_CH
            groups = ((0, o, 1.0, f32),
                      (o, o + MOBA_WIDTH, MOBA_HEAD_DIM ** -0.5, bf16),
                      (o + MOBA_WIDTH, o + 2 * MOBA_WIDTH, 1.0, bf16),
                      (o + 2 * MOBA_WIDTH, o + 3 * MOBA_WIDTH, 1.0, bf16))
            u, q, k, v = _norm_proj(h, norm_mix_g[layer], cd_w_in[j].astype(bf16), groups)
            ya = _conformer_conv(u.reshape(bsz, s, -1), conv_w[j], conv_b[j], conv_ln_g[j], conv_ln_b[j])
            yb = _moba_attention(q.reshape(bsz, s, -1), k.reshape(bsz, s, -1), v.reshape(bsz, s, -1))
            w_out = cd_w_out[j]
        h = _mix_ffn(h, ya.reshape(n, -1), yb.reshape(n, -1), w_out.astype(bf16), norm_ffn_g[layer],
                     ffn_w_gate[layer].astype(bf16), ffn_w_up[layer].astype(bf16),
                     ffn_w_down[layer].astype(bf16), norm_final_g, layer == depth - 1)
    return h.reshape(bsz, s, d)
```

```python
import functools
import math

import jax
import jax.numpy as jnp
from jax import lax
from jax.experimental import pallas as pl
from jax.experimental.pallas import tpu as pltpu

EPS = 1e-6
NEG = -1e30

POOL_GROUP_DIM = 64
POOL_WIDTH = 256
DIFF_HEAD_DIM = 64
DIFF_QK = 768
CONV_CH = 256
CONV_WIDTH = 31
MOBA_HEAD_DIM = 128
MOBA_WIDTH = 768
MOBA_BLOCK = 256
MOBA_TOPK = 3

LANES = 128
SUBLANES = 8
HEAD_COLS = 128
VMEM_LIMIT = 56 * 1024 * 1024

ROW_TILE = 512
KEY_BLOCK = 256
QUERY_TILE = 2 * KEY_BLOCK
CONV_CHUNK = 512
POOL_CHUNK = 512
FFN_CHUNKS = ((0, 768), (768, 1536), (1536, 2304), (2304, 2816))

bf16 = jnp.bfloat16
f32 = jnp.float32


def _cparams(sem):
    return pltpu.CompilerParams(dimension_semantics=sem, vmem_limit_bytes=VMEM_LIMIT)


def _resident(shape):
    nd = len(shape)
    return pl.BlockSpec(shape, lambda *_: (0,) * nd, pipeline_mode=pl.Buffered(1))


def _rms(x, g):
    ms = jnp.mean(x * x, axis=-1, keepdims=True)
    return x * lax.rsqrt(ms + EPS) * g


def _dot_nt(a, b):
    return lax.dot_general(a, b, (((1,), (1,)), ((), ())), preferred_element_type=f32)


def _norm_proj_kernel(groups, x_ref, g_ref, w_ref, wvt_ref, *out_refs):
    xn = _rms(x_ref[...], g_ref[...]).astype(bf16)
    for (a, b, scale, _), o_ref in zip(groups, out_refs[:-1]):
        y = jnp.dot(xn, w_ref[:, a:b], preferred_element_type=f32)
        if scale != 1.0:
            y = y * scale
        o_ref[...] = y.astype(o_ref.dtype)
    vt_ref = out_refs[-1]
    vt = _dot_nt(wvt_ref[...], xn)
    for c in range(ROW_TILE // KEY_BLOCK):
        vt_ref[c] = vt[:, c * KEY_BLOCK:(c + 1) * KEY_BLOCK].astype(vt_ref.dtype)


def _norm_proj(h, g, w, wvt, groups, seq):
    n, d = h.shape
    vw = wvt.shape[0]
    tiles_per_seq = seq // ROW_TILE
    blocks_per_tile = ROW_TILE // KEY_BLOCK
    return pl.pallas_call(
        functools.partial(_norm_proj_kernel, groups),
        grid=(n // ROW_TILE,),
        in_specs=[pl.BlockSpec((ROW_TILE, d), lambda i: (i, 0)),
                  _resident((1, d)),
                  _resident(w.shape),
                  _resident(wvt.shape)],
        out_specs=[pl.BlockSpec((ROW_TILE, b - a), lambda i: (i, 0)) for a, b, _, _ in groups]
        + [pl.BlockSpec((None, blocks_per_tile, vw, KEY_BLOCK),
                        lambda i: (i // tiles_per_seq, i % tiles_per_seq, 0, 0))],
        out_shape=[jax.ShapeDtypeStruct((n, b - a), dt) for a, b, _, dt in groups]
        + [jax.ShapeDtypeStruct((n // seq, seq // KEY_BLOCK, vw, KEY_BLOCK), bf16)],
        compiler_params=_cparams(("parallel",)),
        name="norm_proj",
    )(h, g.reshape(1, d), w, wvt)


POOL_PAD = 16


def _pool_kernel(u_ref, w_ref, scale_ref, o_ref, pad_ref):
    s = u_ref.shape[0]
    pad_ref[0:POOL_PAD, :] = jnp.zeros((POOL_PAD, POOL_WIDTH), f32)
    pad_ref[POOL_PAD:, :] = u_ref[...]
    lane = lax.broadcasted_iota(jnp.int32, (POOL_CHUNK, POOL_WIDTH), 1)
    g0, g1, g2 = (lane < POOL_GROUP_DIM * (i + 1) for i in range(3))
    for c in range(s // POOL_CHUNK):
        base = c * POOL_CHUNK

        def win(k):
            return pad_ref[pl.ds(POOL_PAD + base - k, POOL_CHUNK), :]

        u = win(0)
        pos = base + lax.broadcasted_iota(jnp.int32, (POOL_CHUNK, POOL_WIDTH), 0)
        s2 = u + win(1)
        s4 = s2 + win(2) + win(3)
        s8 = s4 + win(4) + win(5) + win(6) + win(7)
        s16 = s8
        for k in range(8, 16):
            s16 = s16 + win(k)
        tot = jnp.where(g0, s2, jnp.where(g1, s4, jnp.where(g2, s8, s16)))
        wlen = jnp.where(g0, 2, jnp.where(g1, 4, jnp.where(g2, 8, 16)))
        cnt = jnp.minimum(pos + 1, wlen).astype(f32)
        d = (tot / cnt - u).astype(bf16)
        y = jnp.dot(d, w_ref[...], preferred_element_type=f32) * scale_ref[...]
        o_ref[pl.ds(base, POOL_CHUNK), :] = y.astype(o_ref.dtype)


def _pool(u, w_blockdiag, scale):
    b, s, c = u.shape
    return pl.pallas_call(
        _pool_kernel,
        grid=(b,),
        in_specs=[pl.BlockSpec((None, s, c), lambda i: (i, 0, 0)),
                  _resident((c, c)),
                  _resident((1, c))],
        out_specs=pl.BlockSpec((None, s, c), lambda i: (i, 0, 0)),
        out_shape=jax.ShapeDtypeStruct((b, s, c), bf16),
        scratch_shapes=[pltpu.VMEM((s + POOL_PAD, c), f32)],
        compiler_params=_cparams(("parallel",)),
        name="pool",
    )(u, w_blockdiag, scale.reshape(1, c))


LOG2E = math.log2(math.e)
ONES_ROWS = 16
ACC_ROWS = HEAD_COLS + ONES_ROWS


def _with_ones(vt):
    return jnp.concatenate([vt, jnp.ones((ONES_ROWS, vt.shape[1]), vt.dtype)], axis=0)


def _flash_first(sts, vts, acc_ref):
    m = functools.reduce(jnp.maximum, [jnp.max(st, axis=0, keepdims=True) for st in sts])
    acc_ref[...] = sum(jnp.dot(_with_ones(vt), jnp.exp2(st - m).astype(bf16), preferred_element_type=f32)
                       for st, vt in zip(sts, vts))
    return m


def _flash_own_blocks(qk, own0, past_bias, vt_ref, nq, acc_ref):
    t = KEY_BLOCK
    key = lax.broadcasted_iota(jnp.int32, (t, nq), 0)
    qry = lax.broadcasted_iota(jnp.int32, (t, nq), 1)
    second = (qry & t) != 0
    causal = key <= (qry & (t - 1))
    st_a = qk(own0)
    st_past = st_a if past_bias is None else st_a + past_bias
    st_a = jnp.where(second, st_past, jnp.where(causal, st_a, NEG))
    st_b = jnp.where(jnp.logical_and(second, causal), qk(own0 + 1), NEG)
    return _flash_first([st_a, st_b], [vt_ref[own0], vt_ref[own0 + 1]], acc_ref)


def _flash_step(st, vt, bias, m_prev, acc_ref):
    col_max = jnp.max(st, axis=0, keepdims=True)
    if bias is None:
        m_new = jnp.maximum(m_prev, col_max)
        shift = m_new
    else:
        m_new = jnp.maximum(m_prev, col_max + bias)
        shift = m_new - bias
    alpha = jnp.exp2(m_prev - m_new)
    p = jnp.exp2(st - shift).astype(bf16)
    acc_ref[...] = alpha * acc_ref[...] + jnp.dot(_with_ones(vt), p, preferred_element_type=f32)
    return m_new


def _flash_past(npast, qk, vt_ref, bias, st_ref, m, acc_ref):
    st_ref[0] = qk(0)

    def body(i, m):
        j0 = 2 * i
        st_ref[1] = qk(j0 + 1)
        m = _flash_step(st_ref[0], vt_ref[j0], bias(j0), m, acc_ref)
        st_ref[0] = qk(j0 + 2)
        return _flash_step(st_ref[1], vt_ref[j0 + 1], bias(j0 + 1), m, acc_ref)

    return lax.fori_loop(0, lax.shift_right_logical(npast, 1), body, m)


def _diff_kernel(lam_init, q_ref, k_ref, vt_ref, lam_ref, g_ref, o_ref, qq_ref, st_ref, acc_ref):
    t = KEY_BLOCK
    tq = QUERY_TILE
    own0 = 2 * pl.program_id(2)
    q = q_ref[...]
    lane = lax.broadcasted_iota(jnp.int32, (tq, HEAD_COLS), 1)
    zero = jnp.zeros_like(q)
    qq_ref[0:tq, :] = jnp.where(lane < DIFF_HEAD_DIM, q, zero)
    qq_ref[tq:, :] = jnp.where(lane >= DIFF_HEAD_DIM, q, zero)

    def qk(j):
        return _dot_nt(k_ref[pl.ds(pl.multiple_of(j * t, t), t), :], qq_ref[...])

    m = _flash_own_blocks(qk, own0, None, vt_ref, 2 * tq, acc_ref)
    m = _flash_past(own0, qk, vt_ref, lambda j: None, st_ref, m, acc_ref)

    lp = lam_ref[...]
    lam = (jnp.exp(jnp.sum(lp[0:1] * lp[1:2], axis=-1, keepdims=True))
           - jnp.exp(jnp.sum(lp[2:3] * lp[3:4], axis=-1, keepdims=True)) + lam_init)
    ot = acc_ref[0:HEAD_COLS, :] / acc_ref[HEAD_COLS:HEAD_COLS + 1, :]
    o = (ot[:, :tq] - lam * ot[:, tq:]).T
    o_ref[...] = (_rms(o, g_ref[...]) * (1.0 - lam_init)).astype(o_ref.dtype)


def _diff_attention(q, k, vt, lam_params, subln_g, lam_init):
    b, s, w = q.shape
    t = KEY_BLOCK
    tq = QUERY_TILE
    assert s % tq == 0
    heads = w // HEAD_COLS
    tile = pl.BlockSpec((None, tq, HEAD_COLS), lambda bi, h, qi: (bi, qi, h))
    return pl.pallas_call(
        functools.partial(_diff_kernel, lam_init),
        grid=(b, heads, s // tq),
        in_specs=[tile,
                  pl.BlockSpec((None, s, HEAD_COLS), lambda bi, h, qi: (bi, 0, h)),
                  pl.BlockSpec((None, s // t, HEAD_COLS, t), lambda bi, h, qi: (bi, 0, h, 0)),
                  _resident(lam_params.shape), _resident((1, HEAD_COLS))],
        out_specs=tile,
        out_shape=jax.ShapeDtypeStruct((b, s, w), bf16),
        scratch_shapes=[pltpu.VMEM((2 * tq, HEAD_COLS), bf16),
                        pltpu.VMEM((2, t, 2 * tq), f32),
                        pltpu.VMEM((ACC_ROWS, 2 * tq), f32)],
        compiler_params=_cparams(("parallel", "parallel", "arbitrary")),
        name="diff_attn",
    )(q, k, vt, lam_params, subln_g.reshape(1, HEAD_COLS))


def _moba_kernel(q_ref, k_ref, vt_ref, o_ref, kmean_ref, bias_ref, st_ref, acc_ref):
    t = MOBA_BLOCK
    nblk = k_ref.shape[0] // t
    nrow = kmean_ref.shape[0]
    own0 = 2 * pl.program_id(2)

    @pl.when(own0 == 0)
    def _():
        kmean_ref[...] = jnp.zeros_like(kmean_ref)
        for n in range(nblk):
            kb = k_ref[n * t:(n + 1) * t, :].astype(f32)
            kmean_ref[n:n + 1, :] = jnp.mean(kb, axis=0, keepdims=True)

    q = q_ref[...]

    gate = lax.dot_general(kmean_ref[...], q.astype(f32), (((1,), (1,)), ((), ())),
                           preferred_element_type=f32, precision=lax.Precision.HIGHEST)
    blk_i = lax.broadcasted_iota(jnp.int32, (nrow, 2 * t), 0)
    second_g = lax.broadcasted_iota(jnp.int32, (nrow, 2 * t), 1) >= t
    valid = blk_i < jnp.where(second_g, own0 + 1, own0)
    blk = blk_i.astype(f32)
    g = jnp.where(valid, gate, NEG)
    sel = jnp.zeros((nrow, 2 * t), jnp.bool_)
    for _ in range(MOBA_TOPK):
        mx = jnp.max(g, axis=0, keepdims=True)
        first = jnp.min(jnp.where(g == mx, blk, float(nrow)), axis=0, keepdims=True)
        pick = blk == first
        sel = jnp.logical_or(sel, jnp.logical_and(pick, valid))
        g = jnp.where(pick, -jnp.inf, g)
    bias_ref[...] = jnp.where(sel, 0.0, NEG)

    def qk(j):
        return _dot_nt(k_ref[pl.ds(pl.multiple_of(j * t, t), t), :], q)

    m = _flash_own_blocks(qk, own0, bias_ref[pl.ds(own0, 1), :], vt_ref, 2 * t, acc_ref)
    m = _flash_past(own0, qk, vt_ref, lambda j: bias_ref[pl.ds(j, 1), :], st_ref, m, acc_ref)
    o_ref[...] = (acc_ref[0:HEAD_COLS, :] / acc_ref[HEAD_COLS:HEAD_COLS + 1, :]).T.astype(o_ref.dtype)


def _moba_attention(q, k, vt):
    b, s, w = q.shape
    t = MOBA_BLOCK
    assert t == KEY_BLOCK and s % (2 * t) == 0
    nrow = -(-(s // t) // SUBLANES) * SUBLANES
    heads = w // HEAD_COLS
    tile = pl.BlockSpec((None, 2 * t, HEAD_COLS), lambda bi, h, qi: (bi, qi, h))
    return pl.pallas_call(
        _moba_kernel,
        grid=(b, heads, s // (2 * t)),
        in_specs=[tile,
                  pl.BlockSpec((None, s, HEAD_COLS), lambda bi, h, qi: (bi, 0, h)),
                  pl.BlockSpec((None, s // t, HEAD_COLS, t), lambda bi, h, qi: (bi, 0, h, 0))],
        out_specs=tile,
        out_shape=jax.ShapeDtypeStruct((b, s, w), bf16),
        scratch_shapes=[pltpu.VMEM((nrow, HEAD_COLS), f32),
                        pltpu.VMEM((nrow, 2 * t), f32),
                        pltpu.VMEM((2, t, 2 * t), f32),
                        pltpu.VMEM((ACC_ROWS, 2 * t), f32)],
        compiler_params=_cparams(("parallel", "parallel", "arbitrary")),
        name="moba_attn",
    )(q, k, vt)


CONV_PAD = 32


def _conv_kernel(u_ref, w_ref, b_ref, g_ref, beta_ref, o_ref, pad_ref):
    s = u_ref.shape[0]
    pad_ref[0:CONV_PAD, :] = jnp.zeros((CONV_PAD, CONV_CH), f32)
    for c in range(s // CONV_CHUNK):
        rows = pl.ds(c * CONV_CHUNK, CONV_CHUNK)
        u = u_ref[rows, :]
        pad_ref[pl.ds(CONV_PAD + c * CONV_CHUNK, CONV_CHUNK), :] = (
            u[:, :CONV_CH] * jax.nn.sigmoid(u[:, CONV_CH:]))
    w = w_ref[...]
    for c in range(s // CONV_CHUNK):
        base = CONV_PAD + c * CONV_CHUNK - (CONV_WIDTH - 1)
        acc = jnp.zeros((CONV_CHUNK, CONV_CH), f32)
        for j in range(CONV_WIDTH):
            acc = acc + pad_ref[pl.ds(base + j, CONV_CHUNK), :] * w[j:j + 1, :]
        h = acc + b_ref[...]
        mu = jnp.mean(h, axis=-1, keepdims=True)
        var = jnp.mean(jnp.square(h - mu), axis=-1, keepdims=True)
        y = (h - mu) * lax.rsqrt(var + EPS) * g_ref[...] + beta_ref[...]
        o_ref[pl.ds(c * CONV_CHUNK, CONV_CHUNK), :] = (y * jax.nn.sigmoid(y)).astype(o_ref.dtype)


def _conformer_conv(u, conv_w, conv_b, ln_g, ln_b):
    b, s, c2 = u.shape
    c = c2 // 2
    vec = _resident((1, c))
    return pl.pallas_call(
        _conv_kernel,
        grid=(b,),
        in_specs=[pl.BlockSpec((None, s, c2), lambda i: (i, 0, 0)),
                  _resident(conv_w.shape), vec, vec, vec],
        out_specs=pl.BlockSpec((None, s, c), lambda i: (i, 0, 0)),
        out_shape=jax.ShapeDtypeStruct((b, s, c), bf16),
        scratch_shapes=[pltpu.VMEM((s + CONV_PAD, c), f32)],
        compiler_params=_cparams(("parallel",)),
        name="conformer_conv",
    )(u, conv_w, conv_b.reshape(1, c), ln_g.reshape(1, c), ln_b.reshape(1, c))


def _mix_ffn_kernel(final, h_ref, ya_ref, yb_ref, woa_ref, wob_ref, g_ref, wg_ref, wu_ref, wd_ref,
                    gf_ref, o_ref, act_ref):
    h1 = (h_ref[...]
          + jnp.dot(ya_ref[...], woa_ref[...], preferred_element_type=f32)
          + jnp.dot(yb_ref[...], wob_ref[...], preferred_element_type=f32))
    hn = _rms(h1, g_ref[...]).astype(bf16)
    for a, b in FFN_CHUNKS:
        gate = jnp.dot(hn, wg_ref[:, a:b], preferred_element_type=f32)
        up = jnp.dot(hn, wu_ref[:, a:b], preferred_element_type=f32)
        act_ref[:, a:b] = (gate * jax.nn.sigmoid(gate) * up).astype(bf16)
    h2 = h1 + jnp.dot(act_ref[...], wd_ref[...], preferred_element_type=f32)
    if final:
        h2 = _rms(h2, gf_ref[...])
    o_ref[...] = h2


def _mix_ffn(h, ya, yb, w_out, g, wg, wu, wd, g_final, final):
    n, d = h.shape
    ca, cb = ya.shape[1], yb.shape[1]
    dff = wg.shape[1]
    assert FFN_CHUNKS[-1][1] == dff
    woa, wob = w_out[:ca], w_out[ca:]
    row = lambda c: pl.BlockSpec((ROW_TILE, c), lambda i: (i, 0))
    return pl.pallas_call(
        functools.partial(_mix_ffn_kernel, final),
        grid=(n // ROW_TILE,),
        in_specs=[row(d), row(ca), row(cb), _resident(woa.shape), _resident(wob.shape),
                  _resident((1, d)), _resident(wg.shape), _resident(wu.shape), _resident(wd.shape),
                  _resident((1, d))],
        out_specs=row(d),
        out_shape=jax.ShapeDtypeStruct((n, d), f32),
        scratch_shapes=[pltpu.VMEM((ROW_TILE, dff), bf16)],
        compiler_params=_cparams(("parallel",)),
        name="mix_ffn",
    )(h, ya, yb, woa, wob, g.reshape(1, d), wg, wu, wd, g_final.reshape(1, d))


def _block_diag(w):
    g, c, _ = w.shape
    eye = jnp.eye(g, dtype=w.dtype)
    return (eye[:, None, :, None] * w[:, :, None, :]).reshape(g * c, g * c)


def kernel(x, norm_mix_g, norm_ffn_g, norm_final_g, ab_w_in, ab_w_out, pool_w, pool_scale, diff_lambda, diff_subln_g, cd_w_in, cd_w_out, conv_w, conv_b, conv_ln_g, conv_ln_b, ffn_w_gate, ffn_w_up, ffn_w_down):
    bsz, s, d = x.shape
    n = bsz * s
    depth = norm_mix_g.shape[0]
    assert s % ROW_TILE == 0 and s % CONV_CHUNK == 0 and s % POOL_CHUNK == 0
    h = x.reshape(n, d)
    for layer in range(depth):
        j = layer // 2
        if layer % 2 == 0:
            o = POOL_WIDTH
            w_in = ab_w_in[j].astype(bf16)
            groups = ((0, o, 1.0, f32),
                      (o, o + DIFF_QK, DIFF_HEAD_DIM ** -0.5 * LOG2E, bf16),
                      (o + DIFF_QK, o + 2 * DIFF_QK, 1.0, bf16))
            vcol = o + 2 * DIFF_QK
            u, q, k, vt = _norm_proj(h, norm_mix_g[layer], w_in[:, :vcol], w_in[:, vcol:].T, groups, s)
            lam_init = 0.8 - 0.6 * math.exp(-0.3 * layer)
            ya = _pool(u.reshape(bsz, s, -1), _block_diag(pool_w[j]).astype(bf16), pool_scale[j])
            yb = _diff_attention(q.reshape(bsz, s, -1), k.reshape(bsz, s, -1), vt,
                                 diff_lambda[j], diff_subln_g[j], lam_init)
            w_out = ab_w_out[j]
        else:
            o = 2 * CONV_CH
            w_in = cd_w_in[j].astype(bf16)
            groups = ((0, o, 1.0, f32),
                      (o, o + MOBA_WIDTH, MOBA_HEAD_DIM ** -0.5 * LOG2E, bf16),
                      (o + MOBA_WIDTH, o + 2 * MOBA_WIDTH, 1.0, bf16))
            vcol = o + 2 * MOBA_WIDTH
            u, q, k, vt = _norm_proj(h, norm_mix_g[layer], w_in[:, :vcol], w_in[:, vcol:].T, groups, s)
            ya = _conformer_conv(u.reshape(bsz, s, -1), conv_w[j], conv_b[j], conv_ln_g[j], conv_ln_b[j])
            yb = _moba_attention(q.reshape(bsz, s, -1), k.reshape(bsz, s, -1), vt)
            w_out = cd_w_out[j]
        h = _mix_ffn(h, ya.reshape(n, -1), yb.reshape(n, -1), w_out.astype(bf16), norm_ffn_g[layer],
                     ffn_w_gate[layer].astype(bf16), ffn_w_up[layer].astype(bf16),
                     ffn_w_down[layer].astype(bf16), norm_final_g, layer == depth - 1)
    return h.reshape(bsz, s, d)
```

```python
import functools
import math
from typing import Any, Callable, NamedTuple

import jax
import jax.numpy as jnp
from jax import lax
from jax.experimental import pallas as pl
from jax.experimental.pallas import tpu as pltpu

EPS = 1e-6
NEG = -1e30

POOL_GROUP_DIM = 64
POOL_WIDTH = 256
DIFF_HEAD_DIM = 64
DIFF_QK = 768
CONV_CH = 256
CONV_WIDTH = 31
MOBA_HEAD_DIM = 128
MOBA_WIDTH = 768
MOBA_BLOCK = 256
MOBA_TOPK = 3

LANES = 128
SUBLANES = 8
HEAD_COLS = 128
VMEM_LIMIT = 56 * 1024 * 1024

ROW_TILE = 512
KEY_BLOCK = 256
QUERY_TILE = 2 * KEY_BLOCK
HEADS_PER_STEP = 2
CONV_CHUNK = 512
POOL_CHUNK = 512
FFN_CHUNKS = ((0, 768), (768, 1536), (1536, 2304), (2304, 2816))

bf16 = jnp.bfloat16
f32 = jnp.float32


def _cparams(sem):
    return pltpu.CompilerParams(dimension_semantics=sem, vmem_limit_bytes=VMEM_LIMIT)


def _resident(shape, layer=None):
    nd = len(shape)
    if layer is None:
        return pl.BlockSpec(shape, lambda *_: (0,) * nd, pipeline_mode=pl.Buffered(1))
    return pl.BlockSpec((None,) + tuple(shape[1:]), lambda *_: (layer,) + (0,) * (nd - 1),
                        pipeline_mode=pl.Buffered(1))


def _rms(x, g):
    ms = jnp.mean(x * x, axis=-1, keepdims=True)
    return x * lax.rsqrt(ms + EPS) * g


def _dot_nt(a, b):
    return lax.dot_general(a, b, (((1,), (1,)), ((), ())), preferred_element_type=f32)


def _norm_proj_kernel(groups, x_ref, g_ref, w_ref, wvt_ref, *out_refs):
    xn = _rms(x_ref[...], g_ref[...]).astype(bf16)
    for (a, b, scale, _), o_ref in zip(groups, out_refs[:-1]):
        y = jnp.dot(xn, w_ref[:, a:b], preferred_element_type=f32)
        if scale != 1.0:
            y = y * scale
        o_ref[...] = y.astype(o_ref.dtype)
    vt_ref = out_refs[-1]
    vt = _dot_nt(wvt_ref[...], xn)
    for c in range(ROW_TILE // KEY_BLOCK):
        vt_ref[c] = vt[:, c * KEY_BLOCK:(c + 1) * KEY_BLOCK].astype(vt_ref.dtype)


def _norm_proj(h, g, w, wvt, j, groups, seq):
    n, d = h.shape
    vw = wvt.shape[1]
    tiles_per_seq = seq // ROW_TILE
    blocks_per_tile = ROW_TILE // KEY_BLOCK
    return pl.pallas_call(
        functools.partial(_norm_proj_kernel, groups),
        grid=(n // ROW_TILE,),
        in_specs=[pl.BlockSpec((ROW_TILE, d), lambda i: (i, 0)),
                  _resident((1, d)),
                  _resident(w.shape, j),
                  _resident(wvt.shape, j)],
        out_specs=[pl.BlockSpec((ROW_TILE, b - a), lambda i: (i, 0)) for a, b, _, _ in groups]
        + [pl.BlockSpec((None, blocks_per_tile, vw, KEY_BLOCK),
                        lambda i: (i // tiles_per_seq, i % tiles_per_seq, 0, 0))],
        out_shape=[jax.ShapeDtypeStruct((n, b - a), dt) for a, b, _, dt in groups]
        + [jax.ShapeDtypeStruct((n // seq, seq // KEY_BLOCK, vw, KEY_BLOCK), bf16)],
        compiler_params=_cparams(("parallel",)),
        name="norm_proj",
    )(h, g.reshape(1, d), w, wvt)


POOL_PAD = 16


def _pool_kernel(u_ref, w_ref, scale_ref, o_ref, pad_ref):
    s = u_ref.shape[0]
    pad_ref[0:POOL_PAD, :] = jnp.zeros((POOL_PAD, POOL_WIDTH), f32)
    pad_ref[POOL_PAD:, :] = u_ref[...]
    lane = lax.broadcasted_iota(jnp.int32, (POOL_CHUNK, POOL_WIDTH), 1)
    g0, g1, g2 = (lane < POOL_GROUP_DIM * (i + 1) for i in range(3))
    for c in range(s // POOL_CHUNK):
        base = c * POOL_CHUNK

        def win(k):
            return pad_ref[pl.ds(POOL_PAD + base - k, POOL_CHUNK), :]

        u = win(0)
        pos = base + lax.broadcasted_iota(jnp.int32, (POOL_CHUNK, POOL_WIDTH), 0)
        s2 = u + win(1)
        s4 = s2 + win(2) + win(3)
        s8 = s4 + win(4) + win(5) + win(6) + win(7)
        s16 = s8
        for k in range(8, 16):
            s16 = s16 + win(k)
        tot = jnp.where(g0, s2, jnp.where(g1, s4, jnp.where(g2, s8, s16)))
        wlen = jnp.where(g0, 2, jnp.where(g1, 4, jnp.where(g2, 8, 16)))
        cnt = jnp.minimum(pos + 1, wlen).astype(f32)
        d = (tot / cnt - u).astype(bf16)
        y = jnp.dot(d, w_ref[...], preferred_element_type=f32) * scale_ref[...]
        o_ref[pl.ds(base, POOL_CHUNK), :] = y.astype(o_ref.dtype)


def _pool(u, w_blockdiag, scale):
    b, s, c = u.shape
    return pl.pallas_call(
        _pool_kernel,
        grid=(b,),
        in_specs=[pl.BlockSpec((None, s, c), lambda i: (i, 0, 0)),
                  _resident((c, c)),
                  _resident((1, c))],
        out_specs=pl.BlockSpec((None, s, c), lambda i: (i, 0, 0)),
        out_shape=jax.ShapeDtypeStruct((b, s, c), bf16),
        scratch_shapes=[pltpu.VMEM((s + POOL_PAD, c), f32)],
        compiler_params=_cparams(("parallel",)),
        name="pool",
    )(u, w_blockdiag, scale.reshape(1, c))


LOG2E = math.log2(math.e)
ONES_ROWS = 16
ACC_ROWS = HEAD_COLS + ONES_ROWS


def _with_ones(vt):
    return jnp.concatenate([vt, jnp.ones((ONES_ROWS, vt.shape[1]), vt.dtype)], axis=0)


def _flash_first(sts, vts, acc_ref):
    m = functools.reduce(jnp.maximum, [jnp.max(st, axis=0, keepdims=True) for st in sts])
    acc_ref[...] = sum(jnp.dot(_with_ones(vt), jnp.exp2(st - m).astype(bf16), preferred_element_type=f32)
                       for st, vt in zip(sts, vts))
    return m


class _Chain(NamedTuple):
    qk: Callable
    vt: Callable
    bias: Callable
    st_ref: Any
    acc_ref: Any


def _flash_own_blocks(c, own0, past_bias, nq):
    t = KEY_BLOCK
    key = lax.broadcasted_iota(jnp.int32, (t, nq), 0)
    qry = lax.broadcasted_iota(jnp.int32, (t, nq), 1)
    second = (qry & t) != 0
    causal = key <= (qry & (t - 1))
    st_a = c.qk(own0)
    st_past = st_a if past_bias is None else st_a + past_bias
    st_a = jnp.where(second, st_past, jnp.where(causal, st_a, NEG))
    st_b = jnp.where(jnp.logical_and(second, causal), c.qk(own0 + 1), NEG)
    return _flash_first([st_a, st_b], [c.vt(own0), c.vt(own0 + 1)], c.acc_ref)


def _flash_step(st, vt, bias, m_prev, acc_ref):
    col_max = jnp.max(st, axis=0, keepdims=True)
    if bias is None:
        m_new = jnp.maximum(m_prev, col_max)
        shift = m_new
    else:
        m_new = jnp.maximum(m_prev, col_max + bias)
        shift = m_new - bias
    alpha = jnp.exp2(m_prev - m_new)
    p = jnp.exp2(st - shift).astype(bf16)
    acc_ref[...] = alpha * acc_ref[...] + jnp.dot(_with_ones(vt), p, preferred_element_type=f32)
    return m_new


def _flash_past(npast, chains, ms):
    for c in chains:
        c.st_ref[0] = c.qk(0)

    def body(i, ms):
        j0 = 2 * i
        for c in chains:
            c.st_ref[1] = c.qk(j0 + 1)
        ms = [_flash_step(c.st_ref[0], c.vt(j0), c.bias(j0), m, c.acc_ref) for c, m in zip(chains, ms)]
        for c in chains:
            c.st_ref[0] = c.qk(j0 + 2)
        return tuple(_flash_step(c.st_ref[1], c.vt(j0 + 1), c.bias(j0 + 1), m, c.acc_ref)
                     for c, m in zip(chains, ms))

    return lax.fori_loop(0, lax.shift_right_logical(npast, 1), body, tuple(ms))


def _diff_kernel(lam_init, q_ref, k_ref, vt_ref, lam_ref, g_ref, o_ref, qq_ref, st_ref, acc_ref):
    t = KEY_BLOCK
    tq = QUERY_TILE
    own0 = 2 * pl.program_id(2)
    lane = lax.broadcasted_iota(jnp.int32, (tq, HEAD_COLS), 1)
    chains = []
    for h in range(HEADS_PER_STEP):
        cols = slice(h * HEAD_COLS, (h + 1) * HEAD_COLS)
        q = q_ref[:, cols]
        zero = jnp.zeros_like(q)
        qq_ref[h, 0:tq, :] = jnp.where(lane < DIFF_HEAD_DIM, q, zero)
        qq_ref[h, tq:, :] = jnp.where(lane >= DIFF_HEAD_DIM, q, zero)
        chains.append(_Chain(
            qk=lambda j, h=h, cols=cols: _dot_nt(k_ref[pl.ds(pl.multiple_of(j * t, t), t), cols], qq_ref[h]),
            vt=lambda j, cols=cols: vt_ref[j, cols, :],
            bias=lambda j: None,
            st_ref=st_ref.at[h], acc_ref=acc_ref.at[h]))

    ms = [_flash_own_blocks(c, own0, None, 2 * tq) for c in chains]
    _flash_past(own0, chains, ms)

    lp = lam_ref[...]
    lam = (jnp.exp(jnp.sum(lp[0:1] * lp[1:2], axis=-1, keepdims=True))
           - jnp.exp(jnp.sum(lp[2:3] * lp[3:4], axis=-1, keepdims=True)) + lam_init)
    for h, c in enumerate(chains):
        ot = c.acc_ref[0:HEAD_COLS, :] / c.acc_ref[HEAD_COLS:HEAD_COLS + 1, :]
        o = (ot[:, :tq] - lam * ot[:, tq:]).T
        o_ref[:, h * HEAD_COLS:(h + 1) * HEAD_COLS] = (
            _rms(o, g_ref[...]) * (1.0 - lam_init)).astype(o_ref.dtype)


def _attn_specs(b, s, w):
    t, tq, hw = KEY_BLOCK, QUERY_TILE, HEADS_PER_STEP * HEAD_COLS
    assert s % tq == 0 and w % hw == 0
    tile = pl.BlockSpec((None, tq, hw), lambda bi, h, qi: (bi, qi, h))
    k_spec = pl.BlockSpec((None, s, hw), lambda bi, h, qi: (bi, 0, h))
    vt_spec = pl.BlockSpec((None, s // t, hw, t), lambda bi, h, qi: (bi, 0, h, 0))
    return (b, w // hw, s // tq), tile, k_spec, vt_spec


def _diff_attention(q, k, vt, lam_params, subln_g, lam_init):
    b, s, w = q.shape
    grid, tile, k_spec, vt_spec = _attn_specs(b, s, w)
    nq = 2 * QUERY_TILE
    return pl.pallas_call(
        functools.partial(_diff_kernel, lam_init),
        grid=grid,
        in_specs=[tile, k_spec, vt_spec, _resident(lam_params.shape), _resident((1, HEAD_COLS))],
        out_specs=tile,
        out_shape=jax.ShapeDtypeStruct((b, s, w), bf16),
        scratch_shapes=[pltpu.VMEM((HEADS_PER_STEP, nq, HEAD_COLS), bf16),
                        pltpu.VMEM((HEADS_PER_STEP, 2, KEY_BLOCK, nq), f32),
                        pltpu.VMEM((HEADS_PER_STEP, ACC_ROWS, nq), f32)],
        compiler_params=_cparams(("parallel", "parallel", "arbitrary")),
        name="diff_attn",
    )(q, k, vt, lam_params, subln_g.reshape(1, HEAD_COLS))


def _moba_kernel(q_ref, k_ref, vt_ref, o_ref, kmean_ref, bias_ref, st_ref, acc_ref):
    t = MOBA_BLOCK
    nblk = k_ref.shape[0] // t
    nrow = kmean_ref.shape[1]
    own0 = 2 * pl.program_id(2)

    @pl.when(own0 == 0)
    def _():
        kmean_ref[...] = jnp.zeros_like(kmean_ref)
        for h in range(HEADS_PER_STEP):
            for n in range(nblk):
                kb = k_ref[n * t:(n + 1) * t, h * HEAD_COLS:(h + 1) * HEAD_COLS].astype(f32)
                kmean_ref[h, n:n + 1, :] = jnp.mean(kb, axis=0, keepdims=True)

    blk_i = lax.broadcasted_iota(jnp.int32, (nrow, 2 * t), 0)
    second_g = lax.broadcasted_iota(jnp.int32, (nrow, 2 * t), 1) >= t
    valid = blk_i < jnp.where(second_g, own0 + 1, own0)
    blk = blk_i.astype(f32)
    chains = []
    for h in range(HEADS_PER_STEP):
        cols = slice(h * HEAD_COLS, (h + 1) * HEAD_COLS)
        q = q_ref[:, cols]
        gate = lax.dot_general(kmean_ref[h], q.astype(f32), (((1,), (1,)), ((), ())),
                               preferred_element_type=f32, precision=lax.Precision.HIGHEST)
        g = jnp.where(valid, gate, NEG)
        sel = jnp.zeros((nrow, 2 * t), jnp.bool_)
        for _ in range(MOBA_TOPK):
            mx = jnp.max(g, axis=0, keepdims=True)
            first = jnp.min(jnp.where(g == mx, blk, float(nrow)), axis=0, keepdims=True)
            pick = blk == first
            sel = jnp.logical_or(sel, jnp.logical_and(pick, valid))
            g = jnp.where(pick, -jnp.inf, g)
        bias_ref[h] = jnp.where(sel, 0.0, NEG)
        chains.append(_Chain(
            qk=lambda j, cols=cols: _dot_nt(k_ref[pl.ds(pl.multiple_of(j * t, t), t), cols], q_ref[:, cols]),
            vt=lambda j, cols=cols: vt_ref[j, cols, :],
            bias=lambda j, h=h: bias_ref[h, pl.ds(j, 1), :],
            st_ref=st_ref.at[h], acc_ref=acc_ref.at[h]))

    ms = [_flash_own_blocks(c, own0, c.bias(own0), 2 * t) for c in chains]
    _flash_past(own0, chains, ms)
    for h, c in enumerate(chains):
        o_ref[:, h * HEAD_COLS:(h + 1) * HEAD_COLS] = (
            c.acc_ref[0:HEAD_COLS, :] / c.acc_ref[HEAD_COLS:HEAD_COLS + 1, :]).T.astype(o_ref.dtype)


def _moba_attention(q, k, vt):
    b, s, w = q.shape
    assert MOBA_BLOCK == KEY_BLOCK and QUERY_TILE == 2 * MOBA_BLOCK
    nrow = -(-(s // MOBA_BLOCK) // SUBLANES) * SUBLANES
    grid, tile, k_spec, vt_spec = _attn_specs(b, s, w)
    return pl.pallas_call(
        _moba_kernel,
        grid=grid,
        in_specs=[tile, k_spec, vt_spec],
        out_specs=tile,
        out_shape=jax.ShapeDtypeStruct((b, s, w), bf16),
        scratch_shapes=[pltpu.VMEM((HEADS_PER_STEP, nrow, HEAD_COLS), f32),
                        pltpu.VMEM((HEADS_PER_STEP, nrow, QUERY_TILE), f32),
                        pltpu.VMEM((HEADS_PER_STEP, 2, KEY_BLOCK, QUERY_TILE), f32),
                        pltpu.VMEM((HEADS_PER_STEP, ACC_ROWS, QUERY_TILE), f32)],
        compiler_params=_cparams(("parallel", "parallel", "arbitrary")),
        name="moba_attn",
    )(q, k, vt)


CONV_PAD = 32
CONV_HALO = (CONV_WIDTH - 1) // SUBLANES * SUBLANES


def _conv_kernel(u_ref, w_ref, b_ref, g_ref, beta_ref, o_ref, pad_ref, sh_ref):
    s = u_ref.shape[0]
    pad_ref[0:CONV_PAD, :] = jnp.zeros((CONV_PAD, CONV_CH), f32)
    for c in range(s // CONV_CHUNK):
        rows = pl.ds(c * CONV_CHUNK, CONV_CHUNK)
        u = u_ref[rows, :]
        pad_ref[pl.ds(CONV_PAD + c * CONV_CHUNK, CONV_CHUNK), :] = (
            u[:, :CONV_CH] * jax.nn.sigmoid(u[:, CONV_CH:]))
    w = w_ref[...]
    for c in range(s // CONV_CHUNK):
        base = CONV_PAD + c * CONV_CHUNK - CONV_HALO
        for r in range(SUBLANES):
            sh_ref[r] = pad_ref[pl.ds(base - r, CONV_CHUNK + CONV_HALO), :]
        acc = jnp.zeros((CONV_CHUNK, CONV_CH), f32)
        for shift in range(CONV_WIDTH):
            a, r = divmod(shift, SUBLANES)
            j = CONV_WIDTH - 1 - shift
            acc = acc + sh_ref[r, pl.ds(CONV_HALO - SUBLANES * a, CONV_CHUNK), :] * w[j:j + 1, :]
        h = acc + b_ref[...]
        mu = jnp.mean(h, axis=-1, keepdims=True)
        var = jnp.mean(jnp.square(h - mu), axis=-1, keepdims=True)
        y = (h - mu) * lax.rsqrt(var + EPS) * g_ref[...] + beta_ref[...]
        o_ref[pl.ds(c * CONV_CHUNK, CONV_CHUNK), :] = (y * jax.nn.sigmoid(y)).astype(o_ref.dtype)


def _conformer_conv(u, conv_w, conv_b, ln_g, ln_b):
    b, s, c2 = u.shape
    c = c2 // 2
    vec = _resident((1, c))
    return pl.pallas_call(
        _conv_kernel,
        grid=(b,),
        in_specs=[pl.BlockSpec((None, s, c2), lambda i: (i, 0, 0)),
                  _resident(conv_w.shape), vec, vec, vec],
        out_specs=pl.BlockSpec((None, s, c), lambda i: (i, 0, 0)),
        out_shape=jax.ShapeDtypeStruct((b, s, c), bf16),
        scratch_shapes=[pltpu.VMEM((s + CONV_PAD, c), f32),
                        pltpu.VMEM((SUBLANES, CONV_CHUNK + CONV_HALO, c), f32)],
        compiler_params=_cparams(("parallel",)),
        name="conformer_conv",
    )(u, conv_w, conv_b.reshape(1, c), ln_g.reshape(1, c), ln_b.reshape(1, c))


def _mix_ffn_kernel(final, h_ref, ya_ref, yb_ref, wo_ref, g_ref, wg_ref, wu_ref, wd_ref,
                    gf_ref, o_ref, act_ref):
    ca = ya_ref.shape[1]
    h1 = (h_ref[...]
          + jnp.dot(ya_ref[...], wo_ref[0:ca, :], preferred_element_type=f32)
          + jnp.dot(yb_ref[...], wo_ref[ca:, :], preferred_element_type=f32))
    hn = _rms(h1, g_ref[...]).astype(bf16)
    for a, b in FFN_CHUNKS:
        gate = jnp.dot(hn, wg_ref[:, a:b], preferred_element_type=f32)
        up = jnp.dot(hn, wu_ref[:, a:b], preferred_element_type=f32)
        act_ref[:, a:b] = (gate * jax.nn.sigmoid(gate) * up).astype(bf16)
    h2 = h1 + jnp.dot(act_ref[...], wd_ref[...], preferred_element_type=f32)
    if final:
        h2 = _rms(h2, gf_ref[...])
    o_ref[...] = h2


def _mix_ffn(h, ya, yb, w_out, j, g, wg, wu, wd, layer, g_final, final):
    n, d = h.shape
    ca, cb = ya.shape[1], yb.shape[1]
    dff = wg.shape[2]
    assert FFN_CHUNKS[-1][1] == dff and w_out.shape[1] == ca + cb
    row = lambda c: pl.BlockSpec((ROW_TILE, c), lambda i: (i, 0))
    return pl.pallas_call(
        functools.partial(_mix_ffn_kernel, final),
        grid=(n // ROW_TILE,),
        in_specs=[row(d), row(ca), row(cb), _resident(w_out.shape, j),
                  _resident((1, d)), _resident(wg.shape, layer), _resident(wu.shape, layer),
                  _resident(wd.shape, layer), _resident((1, d))],
        out_specs=row(d),
        out_shape=jax.ShapeDtypeStruct((n, d), f32),
        scratch_shapes=[pltpu.VMEM((ROW_TILE, dff), bf16)],
        compiler_params=_cparams(("parallel",)),
        name="mix_ffn",
    )(h, ya, yb, w_out, g.reshape(1, d), wg, wu, wd, g_final.reshape(1, d))


def _block_diag(w):
    g, c, _ = w.shape
    eye = jnp.eye(g, dtype=w.dtype)
    return (eye[:, None, :, None] * w[:, :, None, :]).reshape(g * c, g * c)


def kernel(x, norm_mix_g, norm_ffn_g, norm_final_g, ab_w_in, ab_w_out, pool_w, pool_scale, diff_lambda, diff_subln_g, cd_w_in, cd_w_out, conv_w, conv_b, conv_ln_g, conv_ln_b, ffn_w_gate, ffn_w_up, ffn_w_down):
    bsz, s, d = x.shape
    n = bsz * s
    depth = norm_mix_g.shape[0]
    assert s % ROW_TILE == 0 and s % CONV_CHUNK == 0 and s % POOL_CHUNK == 0
    h = x.reshape(n, d)
    ab_vcol = POOL_WIDTH + 2 * DIFF_QK
    cd_vcol = 2 * CONV_CH + 2 * MOBA_WIDTH
    ab_w, ab_wvt = ab_w_in.astype(bf16), ab_w_in[:, :, ab_vcol:].astype(bf16).transpose(0, 2, 1)
    cd_w, cd_wvt = cd_w_in.astype(bf16), cd_w_in[:, :, cd_vcol:].astype(bf16).transpose(0, 2, 1)
    ab_wo, cd_wo = ab_w_out.astype(bf16), cd_w_out.astype(bf16)
    wg, wu, wd = ffn_w_gate.astype(bf16), ffn_w_up.astype(bf16), ffn_w_down.astype(bf16)
    for layer in range(depth):
        j = layer // 2
        if layer % 2 == 0:
            o = POOL_WIDTH
            groups = ((0, o, 1.0, f32),
                      (o, o + DIFF_QK, DIFF_HEAD_DIM ** -0.5 * LOG2E, bf16),
                      (o + DIFF_QK, ab_vcol, 1.0, bf16))
            u, q, k, vt = _norm_proj(h, norm_mix_g[layer], ab_w, ab_wvt, j, groups, s)
            lam_init = 0.8 - 0.6 * math.exp(-0.3 * layer)
            ya = _pool(u.reshape(bsz, s, -1), _block_diag(pool_w[j]).astype(bf16), pool_scale[j])
            yb = _diff_attention(q.reshape(bsz, s, -1), k.reshape(bsz, s, -1), vt,
                                 diff_lambda[j], diff_subln_g[j], lam_init)
            w_out = ab_wo
        else:
            o = 2 * CONV_CH
            groups = ((0, o, 1.0, f32),
                      (o, o + MOBA_WIDTH, MOBA_HEAD_DIM ** -0.5 * LOG2E, bf16),
                      (o + MOBA_WIDTH, cd_vcol, 1.0, bf16))
            u, q, k, vt = _norm_proj(h, norm_mix_g[layer], cd_w, cd_wvt, j, groups, s)
            ya = _conformer_conv(u.reshape(bsz, s, -1), conv_w[j], conv_b[j], conv_ln_g[j], conv_ln_b[j])
            yb = _moba_attention(q.reshape(bsz, s, -1), k.reshape(bsz, s, -1), vt)
            w_out = cd_wo
        h = _mix_ffn(h, ya.reshape(n, -1), yb.reshape(n, -1), w_out, j, norm_ffn_g[layer],
                     wg, wu, wd, layer, norm_final_g, layer == depth - 1)
    return h.reshape(bsz, s, d)
```

```python
import functools
import math
from typing import Any, Callable, NamedTuple

import jax
import jax.numpy as jnp
from jax import lax
from jax.experimental import pallas as pl
from jax.experimental.pallas import tpu as pltpu

EPS = 1e-6
NEG = -1e30

POOL_GROUP_DIM = 64
POOL_WIDTH = 256
DIFF_HEAD_DIM = 64
DIFF_QK = 768
CONV_CH = 256
CONV_WIDTH = 31
MOBA_HEAD_DIM = 128
MOBA_WIDTH = 768
MOBA_BLOCK = 256
MOBA_TOPK = 3

LANES = 128
SUBLANES = 8
HEAD_COLS = 128
VMEM_LIMIT = 56 * 1024 * 1024

ROW_TILE = 512
KEY_BLOCK = 256
QUERY_TILE = 2 * KEY_BLOCK
HEADS_PER_STEP = 2
CONV_CHUNK = 512
POOL_CHUNK = 512
FFN_CHUNKS = ((0, 768), (768, 1536), (1536, 2304), (2304, 2816))

bf16 = jnp.bfloat16
f32 = jnp.float32


def _cparams(sem):
    return pltpu.CompilerParams(dimension_semantics=sem, vmem_limit_bytes=VMEM_LIMIT)


def _resident(shape, layer=None):
    nd = len(shape)
    if layer is None:
        return pl.BlockSpec(shape, lambda *_: (0,) * nd, pipeline_mode=pl.Buffered(1))
    return pl.BlockSpec((None,) + tuple(shape[1:]), lambda *_: (layer,) + (0,) * (nd - 1),
                        pipeline_mode=pl.Buffered(1))


def _rms(x, g):
    ms = jnp.mean(x * x, axis=-1, keepdims=True)
    return x * lax.rsqrt(ms + EPS) * g


def _dot_nt(a, b):
    return lax.dot_general(a, b, (((1,), (1,)), ((), ())), preferred_element_type=f32)


def _norm_proj_kernel(groups, x_ref, g_ref, w_ref, wvt_ref, *out_refs):
    xn = _rms(x_ref[...], g_ref[...]).astype(bf16)
    for (a, b, scale, _), o_ref in zip(groups, out_refs[:-1]):
        y = jnp.dot(xn, w_ref[:, a:b], preferred_element_type=f32)
        if scale != 1.0:
            y = y * scale
        o_ref[...] = y.astype(o_ref.dtype)
    vt_ref = out_refs[-1]
    vt = _dot_nt(wvt_ref[...], xn)
    for c in range(ROW_TILE // KEY_BLOCK):
        vt_ref[c] = vt[:, c * KEY_BLOCK:(c + 1) * KEY_BLOCK].astype(vt_ref.dtype)


def _norm_proj(h, g, w, wvt, j, groups, seq):
    n, d = h.shape
    vw = wvt.shape[1]
    tiles_per_seq = seq // ROW_TILE
    blocks_per_tile = ROW_TILE // KEY_BLOCK
    return pl.pallas_call(
        functools.partial(_norm_proj_kernel, groups),
        grid=(n // ROW_TILE,),
        in_specs=[pl.BlockSpec((ROW_TILE, d), lambda i: (i, 0)),
                  _resident((1, d)),
                  _resident(w.shape, j),
                  _resident(wvt.shape, j)],
        out_specs=[pl.BlockSpec((ROW_TILE, b - a), lambda i: (i, 0)) for a, b, _, _ in groups]
        + [pl.BlockSpec((None, blocks_per_tile, vw, KEY_BLOCK),
                        lambda i: (i // tiles_per_seq, i % tiles_per_seq, 0, 0))],
        out_shape=[jax.ShapeDtypeStruct((n, b - a), dt) for a, b, _, dt in groups]
        + [jax.ShapeDtypeStruct((n // seq, seq // KEY_BLOCK, vw, KEY_BLOCK), bf16)],
        compiler_params=_cparams(("parallel",)),
        name="norm_proj",
    )(h, g.reshape(1, d), w, wvt)


POOL_PAD = 16


def _pool_kernel(u_ref, w_ref, scale_ref, o_ref, pad_ref):
    s = u_ref.shape[0]
    pad_ref[0:POOL_PAD, :] = jnp.zeros((POOL_PAD, POOL_WIDTH), f32)
    pad_ref[POOL_PAD:, :] = u_ref[...]
    lane = lax.broadcasted_iota(jnp.int32, (POOL_CHUNK, POOL_WIDTH), 1)
    g0, g1, g2 = (lane < POOL_GROUP_DIM * (i + 1) for i in range(3))
    for c in range(s // POOL_CHUNK):
        base = c * POOL_CHUNK

        def win(k):
            return pad_ref[pl.ds(POOL_PAD + base - k, POOL_CHUNK), :]

        u = win(0)
        pos = base + lax.broadcasted_iota(jnp.int32, (POOL_CHUNK, POOL_WIDTH), 0)
        s2 = u + win(1)
        s4 = s2 + win(2) + win(3)
        s8 = s4 + win(4) + win(5) + win(6) + win(7)
        s16 = s8
        for k in range(8, 16):
            s16 = s16 + win(k)
        tot = jnp.where(g0, s2, jnp.where(g1, s4, jnp.where(g2, s8, s16)))
        wlen = jnp.where(g0, 2, jnp.where(g1, 4, jnp.where(g2, 8, 16)))
        cnt = jnp.minimum(pos + 1, wlen).astype(f32)
        d = (tot / cnt - u).astype(bf16)
        y = jnp.dot(d, w_ref[...], preferred_element_type=f32) * scale_ref[...]
        o_ref[pl.ds(base, POOL_CHUNK), :] = y.astype(o_ref.dtype)


def _pool(u, w_blockdiag, scale):
    b, s, c = u.shape
    return pl.pallas_call(
        _pool_kernel,
        grid=(b,),
        in_specs=[pl.BlockSpec((None, s, c), lambda i: (i, 0, 0)),
                  _resident((c, c)),
                  _resident((1, c))],
        out_specs=pl.BlockSpec((None, s, c), lambda i: (i, 0, 0)),
        out_shape=jax.ShapeDtypeStruct((b, s, c), bf16),
        scratch_shapes=[pltpu.VMEM((s + POOL_PAD, c), f32)],
        compiler_params=_cparams(("parallel",)),
        name="pool",
    )(u, w_blockdiag, scale.reshape(1, c))


LOG2E = math.log2(math.e)
ONES_ROWS = 16
ACC_ROWS = HEAD_COLS + ONES_ROWS


def _with_ones(vt):
    return jnp.concatenate([vt, jnp.ones((ONES_ROWS, vt.shape[1]), vt.dtype)], axis=0)


class _Chain(NamedTuple):
    qk: Callable
    vt: Callable
    bias: Callable
    st_ref: Any
    acc_ref: Any


def _own_masks(mask_ref, nq):
    t = KEY_BLOCK
    key = lax.broadcasted_iota(jnp.int32, (t, nq), 0)
    qry = lax.broadcasted_iota(jnp.int32, (t, nq), 1)
    second = (qry & t) != 0
    causal = key <= (qry & (t - 1))
    mask_ref[0] = jnp.where(jnp.logical_or(second, causal), 0.0, NEG)
    mask_ref[1] = jnp.where(jnp.logical_and(second, causal), 0.0, NEG)


def _flash_step(st, vt, bias, m_prev, acc_ref):
    col_max = jnp.max(st, axis=0, keepdims=True)
    if bias is None:
        m_new = jnp.maximum(m_prev, col_max)
        shift = m_new
    else:
        m_new = jnp.maximum(m_prev, col_max + bias)
        shift = m_new - bias
    alpha = jnp.exp2(m_prev - m_new)
    p = jnp.exp2(st - shift).astype(bf16)
    acc_ref[...] = alpha * acc_ref[...] + jnp.dot(_with_ones(vt), p, preferred_element_type=f32)
    return m_new


def _flash_tile(qi, chains, mask_ref, nq):
    own0 = 2 * qi
    ms = [jnp.full((1, nq), NEG, f32) for _ in chains]
    for c in chains:
        c.acc_ref[...] = jnp.zeros_like(c.acc_ref)
        c.st_ref[0] = c.qk(own0 + 1)
    for c in chains:
        c.st_ref[1] = c.qk(own0)
    ms = [_flash_step(c.st_ref[0] + mask_ref[1], c.vt(own0 + 1), c.bias(own0 + 1), m, c.acc_ref)
          for c, m in zip(chains, ms)]
    for c in chains:
        c.st_ref[0] = c.qk(0)
    ms = [_flash_step(c.st_ref[1] + mask_ref[0], c.vt(own0), c.bias(own0), m, c.acc_ref)
          for c, m in zip(chains, ms)]

    def body(i, ms):
        j0 = 2 * i
        for c in chains:
            c.st_ref[1] = c.qk(j0 + 1)
        ms = [_flash_step(c.st_ref[0], c.vt(j0), c.bias(j0), m, c.acc_ref) for c, m in zip(chains, ms)]
        for c in chains:
            c.st_ref[0] = c.qk(j0 + 2)
        return tuple(_flash_step(c.st_ref[1], c.vt(j0 + 1), c.bias(j0 + 1), m, c.acc_ref)
                     for c, m in zip(chains, ms))

    lax.fori_loop(0, qi, body, tuple(ms))


def _diff_kernel(lam_init, q_ref, k_ref, vt_ref, lam_ref, g_ref, o_ref, qq_ref, mask_ref, st_ref, acc_ref):
    t = KEY_BLOCK
    tq = QUERY_TILE
    nq = 2 * tq
    _own_masks(mask_ref, nq)
    lane = lax.broadcasted_iota(jnp.int32, (tq, HEAD_COLS), 1)
    lp = lam_ref[...]
    lam = (jnp.exp(jnp.sum(lp[0:1] * lp[1:2], axis=-1, keepdims=True))
           - jnp.exp(jnp.sum(lp[2:3] * lp[3:4], axis=-1, keepdims=True)) + lam_init)
    chains = [_Chain(
        qk=lambda j, h=h: _dot_nt(k_ref[pl.ds(pl.multiple_of(j * t, t), t), _head_cols(h)], qq_ref[h]),
        vt=lambda j, h=h: vt_ref[j, _head_cols(h), :],
        bias=lambda j: None,
        st_ref=st_ref.at[h], acc_ref=acc_ref.at[h]) for h in range(HEADS_PER_STEP)]

    def tile(qi, carry):
        rows = pl.ds(pl.multiple_of(qi * tq, tq), tq)
        for h in range(HEADS_PER_STEP):
            q = q_ref[rows, _head_cols(h)]
            zero = jnp.zeros_like(q)
            qq_ref[h, 0:tq, :] = jnp.where(lane < DIFF_HEAD_DIM, q, zero)
            qq_ref[h, tq:, :] = jnp.where(lane >= DIFF_HEAD_DIM, q, zero)
        _flash_tile(qi, chains, mask_ref, nq)
        for h, c in enumerate(chains):
            ot = c.acc_ref[0:HEAD_COLS, :] / c.acc_ref[HEAD_COLS:HEAD_COLS + 1, :]
            o = (ot[:, :tq] - lam * ot[:, tq:]).T
            o_ref[rows, _head_cols(h)] = (_rms(o, g_ref[...]) * (1.0 - lam_init)).astype(o_ref.dtype)
        return carry

    lax.fori_loop(0, q_ref.shape[0] // tq, tile, 0)


def _head_cols(h):
    return slice(h * HEAD_COLS, (h + 1) * HEAD_COLS)


def _attn_specs(b, s, w):
    hw = HEADS_PER_STEP * HEAD_COLS
    assert s % QUERY_TILE == 0 and w % hw == 0
    seq_spec = pl.BlockSpec((None, s, hw), lambda bi, h: (bi, 0, h))
    vt_spec = pl.BlockSpec((None, s // KEY_BLOCK, hw, KEY_BLOCK), lambda bi, h: (bi, 0, h, 0))
    return (b, w // hw), seq_spec, vt_spec


def _diff_attention(q, k, vt, lam_params, subln_g, lam_init):
    b, s, w = q.shape
    grid, seq_spec, vt_spec = _attn_specs(b, s, w)
    nq = 2 * QUERY_TILE
    return pl.pallas_call(
        functools.partial(_diff_kernel, lam_init),
        grid=grid,
        in_specs=[seq_spec, seq_spec, vt_spec, _resident(lam_params.shape), _resident((1, HEAD_COLS))],
        out_specs=seq_spec,
        out_shape=jax.ShapeDtypeStruct((b, s, w), bf16),
        scratch_shapes=[pltpu.VMEM((HEADS_PER_STEP, nq, HEAD_COLS), bf16),
                        pltpu.VMEM((2, KEY_BLOCK, nq), f32),
                        pltpu.VMEM((HEADS_PER_STEP, 2, KEY_BLOCK, nq), f32),
                        pltpu.VMEM((HEADS_PER_STEP, ACC_ROWS, nq), f32)],
        compiler_params=_cparams(("parallel", "parallel")),
        name="diff_attn",
    )(q, k, vt, lam_params, subln_g.reshape(1, HEAD_COLS))


def _moba_kernel(q_ref, k_ref, vt_ref, o_ref, kmean_ref, bias_ref, mask_ref, st_ref, acc_ref):
    t = MOBA_BLOCK
    tq = QUERY_TILE
    nblk = k_ref.shape[0] // t
    nrow = kmean_ref.shape[1]
    _own_masks(mask_ref, tq)
    kmean_ref[...] = jnp.zeros_like(kmean_ref)
    for h in range(HEADS_PER_STEP):
        for n in range(nblk):
            kb = k_ref[n * t:(n + 1) * t, _head_cols(h)].astype(f32)
            kmean_ref[h, n:n + 1, :] = jnp.mean(kb, axis=0, keepdims=True)
    blk_i = lax.broadcasted_iota(jnp.int32, (nrow, tq), 0)
    second = lax.broadcasted_iota(jnp.int32, (nrow, tq), 1) >= t
    blk = blk_i.astype(f32)

    def tile(qi, carry):
        own0 = 2 * qi
        rows = pl.ds(pl.multiple_of(qi * tq, tq), tq)
        valid = blk_i < jnp.where(second, own0 + 1, own0)
        for h in range(HEADS_PER_STEP):
            gate = lax.dot_general(kmean_ref[h], q_ref[rows, _head_cols(h)].astype(f32),
                                   (((1,), (1,)), ((), ())),
                                   preferred_element_type=f32, precision=lax.Precision.HIGHEST)
            g = jnp.where(valid, gate, NEG)
            sel = jnp.zeros((nrow, tq), jnp.bool_)
            for _ in range(MOBA_TOPK):
                mx = jnp.max(g, axis=0, keepdims=True)
                first = jnp.min(jnp.where(g == mx, blk, float(nrow)), axis=0, keepdims=True)
                pick = blk == first
                sel = jnp.logical_or(sel, jnp.logical_and(pick, valid))
                g = jnp.where(pick, -jnp.inf, g)
            own_free = jnp.logical_or(blk_i == own0 + 1,
                                      jnp.logical_and(blk_i == own0, jnp.logical_not(second)))
            bias_ref[h] = jnp.where(jnp.logical_or(sel, own_free), 0.0, NEG)
        chains = [_Chain(
            qk=lambda j, h=h: _dot_nt(k_ref[pl.ds(pl.multiple_of(j * t, t), t), _head_cols(h)],
                                      q_ref[rows, _head_cols(h)]),
            vt=lambda j, h=h: vt_ref[j, _head_cols(h), :],
            bias=lambda j, h=h: bias_ref[h, pl.ds(j, 1), :],
            st_ref=st_ref.at[h], acc_ref=acc_ref.at[h]) for h in range(HEADS_PER_STEP)]
        _flash_tile(qi, chains, mask_ref, tq)
        for h, c in enumerate(chains):
            o_ref[rows, _head_cols(h)] = (
                c.acc_ref[0:HEAD_COLS, :] / c.acc_ref[HEAD_COLS:HEAD_COLS + 1, :]).T.astype(o_ref.dtype)
        return carry

    lax.fori_loop(0, q_ref.shape[0] // tq, tile, 0)


def _moba_attention(q, k, vt):
    b, s, w = q.shape
    assert MOBA_BLOCK == KEY_BLOCK and QUERY_TILE == 2 * MOBA_BLOCK
    nrow = -(-(s // MOBA_BLOCK) // SUBLANES) * SUBLANES
    grid, seq_spec, vt_spec = _attn_specs(b, s, w)
    return pl.pallas_call(
        _moba_kernel,
        grid=grid,
        in_specs=[seq_spec, seq_spec, vt_spec],
        out_specs=seq_spec,
        out_shape=jax.ShapeDtypeStruct((b, s, w), bf16),
        scratch_shapes=[pltpu.VMEM((HEADS_PER_STEP, nrow, HEAD_COLS), f32),
                        pltpu.VMEM((HEADS_PER_STEP, nrow, QUERY_TILE), f32),
                        pltpu.VMEM((2, KEY_BLOCK, QUERY_TILE), f32),
                        pltpu.VMEM((HEADS_PER_STEP, 2, KEY_BLOCK, QUERY_TILE), f32),
                        pltpu.VMEM((HEADS_PER_STEP, ACC_ROWS, QUERY_TILE), f32)],
        compiler_params=_cparams(("parallel", "parallel")),
        name="moba_attn",
    )(q, k, vt)


CONV_PAD = 32
CONV_HALO = (CONV_WIDTH - 1) // SUBLANES * SUBLANES


def _conv_kernel(u_ref, w_ref, b_ref, g_ref, beta_ref, o_ref, pad_ref, sh_ref):
    s = u_ref.shape[0]
    pad_ref[0:CONV_PAD, :] = jnp.zeros((CONV_PAD, CONV_CH), f32)
    for c in range(s // CONV_CHUNK):
        rows = pl.ds(c * CONV_CHUNK, CONV_CHUNK)
        u = u_ref[rows, :]
        pad_ref[pl.ds(CONV_PAD + c * CONV_CHUNK, CONV_CHUNK), :] = (
            u[:, :CONV_CH] * jax.nn.sigmoid(u[:, CONV_CH:]))
    w = w_ref[...]
    for c in range(s // CONV_CHUNK):
        base = CONV_PAD + c * CONV_CHUNK - CONV_HALO
        for r in range(SUBLANES):
            sh_ref[r] = pad_ref[pl.ds(base - r, CONV_CHUNK + CONV_HALO), :]
        acc = jnp.zeros((CONV_CHUNK, CONV_CH), f32)
        for shift in range(CONV_WIDTH):
            a, r = divmod(shift, SUBLANES)
            j = CONV_WIDTH - 1 - shift
            acc = acc + sh_ref[r, pl.ds(CONV_HALO - SUBLANES * a, CONV_CHUNK), :] * w[j:j + 1, :]
        h = acc + b_ref[...]
        mu = jnp.mean(h, axis=-1, keepdims=True)
        var = jnp.mean(jnp.square(h - mu), axis=-1, keepdims=True)
        y = (h - mu) * lax.rsqrt(var + EPS) * g_ref[...] + beta_ref[...]
        o_ref[pl.ds(c * CONV_CHUNK, CONV_CHUNK), :] = (y * jax.nn.sigmoid(y)).astype(o_ref.dtype)


def _conformer_conv(u, conv_w, conv_b, ln_g, ln_b):
    b, s, c2 = u.shape
    c = c2 // 2
    vec = _resident((1, c))
    return pl.pallas_call(
        _conv_kernel,
        grid=(b,),
        in_specs=[pl.BlockSpec((None, s, c2), lambda i: (i, 0, 0)),
                  _resident(conv_w.shape), vec, vec, vec],
        out_specs=pl.BlockSpec((None, s, c), lambda i: (i, 0, 0)),
        out_shape=jax.ShapeDtypeStruct((b, s, c), bf16),
        scratch_shapes=[pltpu.VMEM((s + CONV_PAD, c), f32),
                        pltpu.VMEM((SUBLANES, CONV_CHUNK + CONV_HALO, c), f32)],
        compiler_params=_cparams(("parallel",)),
        name="conformer_conv",
    )(u, conv_w, conv_b.reshape(1, c), ln_g.reshape(1, c), ln_b.reshape(1, c))


def _mix_ffn_kernel(final, h_ref, ya_ref, yb_ref, wo_ref, g_ref, wg_ref, wu_ref, wd_ref,
                    gf_ref, o_ref, act_ref):
    ca = ya_ref.shape[1]
    h1 = (h_ref[...]
          + jnp.dot(ya_ref[...], wo_ref[0:ca, :], preferred_element_type=f32)
          + jnp.dot(yb_ref[...], wo_ref[ca:, :], preferred_element_type=f32))
    hn = _rms(h1, g_ref[...]).astype(bf16)
    for a, b in FFN_CHUNKS:
        gate = jnp.dot(hn, wg_ref[:, a:b], preferred_element_type=f32)
        up = jnp.dot(hn, wu_ref[:, a:b], preferred_element_type=f32)
        act_ref[:, a:b] = (gate * jax.nn.sigmoid(gate) * up).astype(bf16)
    h2 = h1 + jnp.dot(act_ref[...], wd_ref[...], preferred_element_type=f32)
    if final:
        h2 = _rms(h2, gf_ref[...])
    o_ref[...] = h2


def _mix_ffn(h, ya, yb, w_out, j, g, wg, wu, wd, layer, g_final, final):
    n, d = h.shape
    ca, cb = ya.shape[1], yb.shape[1]
    dff = wg.shape[2]
    assert FFN_CHUNKS[-1][1] == dff and w_out.shape[1] == ca + cb
    row = lambda c: pl.BlockSpec((ROW_TILE, c), lambda i: (i, 0))
    return pl.pallas_call(
        functools.partial(_mix_ffn_kernel, final),
        grid=(n // ROW_TILE,),
        in_specs=[row(d), row(ca), row(cb), _resident(w_out.shape, j),
                  _resident((1, d)), _resident(wg.shape, layer), _resident(wu.shape, layer),
                  _resident(wd.shape, layer), _resident((1, d))],
        out_specs=row(d),
        out_shape=jax.ShapeDtypeStruct((n, d), f32),
        scratch_shapes=[pltpu.VMEM((ROW_TILE, dff), bf16)],
        compiler_params=_cparams(("parallel",)),
        name="mix_ffn",
    )(h, ya, yb, w_out, g.reshape(1, d), wg, wu, wd, g_final.reshape(1, d))


def _block_diag(w):
    g, c, _ = w.shape
    eye = jnp.eye(g, dtype=w.dtype)
    return (eye[:, None, :, None] * w[:, :, None, :]).reshape(g * c, g * c)


def kernel(x, norm_mix_g, norm_ffn_g, norm_final_g, ab_w_in, ab_w_out, pool_w, pool_scale, diff_lambda, diff_subln_g, cd_w_in, cd_w_out, conv_w, conv_b, conv_ln_g, conv_ln_b, ffn_w_gate, ffn_w_up, ffn_w_down):
    bsz, s, d = x.shape
    n = bsz * s
    depth = norm_mix_g.shape[0]
    assert s % ROW_TILE == 0 and s % CONV_CHUNK == 0 and s % POOL_CHUNK == 0
    h = x.reshape(n, d)
    ab_vcol = POOL_WIDTH + 2 * DIFF_QK
    cd_vcol = 2 * CONV_CH + 2 * MOBA_WIDTH
    ab_w, ab_wvt = ab_w_in.astype(bf16), ab_w_in[:, :, ab_vcol:].astype(bf16).transpose(0, 2, 1)
    cd_w, cd_wvt = cd_w_in.astype(bf16), cd_w_in[:, :, cd_vcol:].astype(bf16).transpose(0, 2, 1)
    ab_wo, cd_wo = ab_w_out.astype(bf16), cd_w_out.astype(bf16)
    wg, wu, wd = ffn_w_gate.astype(bf16), ffn_w_up.astype(bf16), ffn_w_down.astype(bf16)
    for layer in range(depth):
        j = layer // 2
        if layer % 2 == 0:
            o = POOL_WIDTH
            groups = ((0, o, 1.0, f32),
                      (o, o + DIFF_QK, DIFF_HEAD_DIM ** -0.5 * LOG2E, bf16),
                      (o + DIFF_QK, ab_vcol, 1.0, bf16))
            u, q, k, vt = _norm_proj(h, norm_mix_g[layer], ab_w, ab_wvt, j, groups, s)
            lam_init = 0.8 - 0.6 * math.exp(-0.3 * layer)
            ya = _pool(u.reshape(bsz, s, -1), _block_diag(pool_w[j]).astype(bf16), pool_scale[j])
            yb = _diff_attention(q.reshape(bsz, s, -1), k.reshape(bsz, s, -1), vt,
                                 diff_lambda[j], diff_subln_g[j], lam_init)
            w_out = ab_wo
        else:
            o = 2 * CONV_CH
            groups = ((0, o, 1.0, f32),
                      (o, o + MOBA_WIDTH, MOBA_HEAD_DIM ** -0.5 * LOG2E, bf16),
                      (o + MOBA_WIDTH, cd_vcol, 1.0, bf16))
            u, q, k, vt = _norm_proj(h, norm_mix_g[layer], cd_w, cd_wvt, j, groups, s)
            ya = _conformer_conv(u.reshape(bsz, s, -1), conv_w[j], conv_b[j], conv_ln_g[j], conv_ln_b[j])
            yb = _moba_attention(q.reshape(bsz, s, -1), k.reshape(bsz, s, -1), vt)
            w_out = cd_wo
        h = _mix_ffn(h, ya.reshape(n, -1), yb.reshape(n, -1), w_out, j, norm_ffn_g[layer],
                     wg, wu, wd, layer, norm_final_g, layer == depth - 1)
    return h.reshape(bsz, s, d)
```

```python
import functools
import math
from typing import Any, Callable, NamedTuple

import jax
import jax.numpy as jnp
from jax import lax
from jax.experimental import pallas as pl
from jax.experimental.pallas import tpu as pltpu

EPS = 1e-6
NEG = -1e30

POOL_GROUP_DIM = 64
POOL_WIDTH = 256
DIFF_HEAD_DIM = 64
DIFF_QK = 768
CONV_CH = 256
CONV_WIDTH = 31
MOBA_HEAD_DIM = 128
MOBA_WIDTH = 768
MOBA_BLOCK = 256
MOBA_TOPK = 3

LANES = 128
SUBLANES = 8
HEAD_COLS = 128
VMEM_LIMIT = 56 * 1024 * 1024

ROW_TILE = 512
KEY_BLOCK = 256
QUERY_TILE = 2 * KEY_BLOCK
DIFF_HEADS_PER_STEP = 2
MOBA_HEADS_PER_STEP = 3
CONV_CHUNK = 512
POOL_CHUNK = 512
FFN_CHUNKS = ((0, 768), (768, 1536), (1536, 2304), (2304, 2816))

bf16 = jnp.bfloat16
f32 = jnp.float32


def _cparams(sem):
    return pltpu.CompilerParams(dimension_semantics=sem, vmem_limit_bytes=VMEM_LIMIT)


def _resident(shape, layer=None):
    nd = len(shape)
    if layer is None:
        return pl.BlockSpec(shape, lambda *_: (0,) * nd, pipeline_mode=pl.Buffered(1))
    return pl.BlockSpec((None,) + tuple(shape[1:]), lambda *_: (layer,) + (0,) * (nd - 1),
                        pipeline_mode=pl.Buffered(1))


def _rms(x, g):
    ms = jnp.mean(x * x, axis=-1, keepdims=True)
    return x * lax.rsqrt(ms + EPS) * g


def _dot_nt(a, b):
    return lax.dot_general(a, b, (((1,), (1,)), ((), ())), preferred_element_type=f32)


def _norm_proj_kernel(groups, x_ref, g_ref, w_ref, wvt_ref, *out_refs):
    xn = _rms(x_ref[...], g_ref[...]).astype(bf16)
    for (a, b, scale, _), o_ref in zip(groups, out_refs[:-1]):
        y = jnp.dot(xn, w_ref[:, a:b], preferred_element_type=f32)
        if scale != 1.0:
            y = y * scale
        o_ref[...] = y.astype(o_ref.dtype)
    vt_ref = out_refs[-1]
    vt = _dot_nt(wvt_ref[...], xn)
    for c in range(ROW_TILE // KEY_BLOCK):
        vt_ref[c] = vt[:, c * KEY_BLOCK:(c + 1) * KEY_BLOCK].astype(vt_ref.dtype)


def _norm_proj(h, g, w, wvt, j, groups, seq):
    n, d = h.shape
    vw = wvt.shape[1]
    tiles_per_seq = seq // ROW_TILE
    blocks_per_tile = ROW_TILE // KEY_BLOCK
    return pl.pallas_call(
        functools.partial(_norm_proj_kernel, groups),
        grid=(n // ROW_TILE,),
        in_specs=[pl.BlockSpec((ROW_TILE, d), lambda i: (i, 0)),
                  _resident((1, d)),
                  _resident(w.shape, j),
                  _resident(wvt.shape, j)],
        out_specs=[pl.BlockSpec((ROW_TILE, b - a), lambda i: (i, 0)) for a, b, _, _ in groups]
        + [pl.BlockSpec((None, blocks_per_tile, vw, KEY_BLOCK),
                        lambda i: (i // tiles_per_seq, i % tiles_per_seq, 0, 0))],
        out_shape=[jax.ShapeDtypeStruct((n, b - a), dt) for a, b, _, dt in groups]
        + [jax.ShapeDtypeStruct((n // seq, seq // KEY_BLOCK, vw, KEY_BLOCK), bf16)],
        compiler_params=_cparams(("parallel",)),
        name="norm_proj",
    )(h, g.reshape(1, d), w, wvt)


POOL_PAD = 16


def _pool_kernel(u_ref, w_ref, scale_ref, o_ref, pad_ref):
    s = u_ref.shape[0]
    pad_ref[0:POOL_PAD, :] = jnp.zeros((POOL_PAD, POOL_WIDTH), f32)
    pad_ref[POOL_PAD:, :] = u_ref[...]
    lane = lax.broadcasted_iota(jnp.int32, (POOL_CHUNK, POOL_WIDTH), 1)
    g0, g1, g2 = (lane < POOL_GROUP_DIM * (i + 1) for i in range(3))
    for c in range(s // POOL_CHUNK):
        base = c * POOL_CHUNK

        def win(k):
            return pad_ref[pl.ds(POOL_PAD + base - k, POOL_CHUNK), :]

        u = win(0)
        pos = base + lax.broadcasted_iota(jnp.int32, (POOL_CHUNK, POOL_WIDTH), 0)
        s2 = u + win(1)
        s4 = s2 + win(2) + win(3)
        s8 = s4 + win(4) + win(5) + win(6) + win(7)
        s16 = s8
        for k in range(8, 16):
            s16 = s16 + win(k)
        tot = jnp.where(g0, s2, jnp.where(g1, s4, jnp.where(g2, s8, s16)))
        wlen = jnp.where(g0, 2, jnp.where(g1, 4, jnp.where(g2, 8, 16)))
        cnt = jnp.minimum(pos + 1, wlen).astype(f32)
        d = (tot / cnt - u).astype(bf16)
        y = jnp.dot(d, w_ref[...], preferred_element_type=f32) * scale_ref[...]
        o_ref[pl.ds(base, POOL_CHUNK), :] = y.astype(o_ref.dtype)


def _pool(u, w_blockdiag, scale):
    b, s, c = u.shape
    return pl.pallas_call(
        _pool_kernel,
        grid=(b,),
        in_specs=[pl.BlockSpec((None, s, c), lambda i: (i, 0, 0)),
                  _resident((c, c)),
                  _resident((1, c))],
        out_specs=pl.BlockSpec((None, s, c), lambda i: (i, 0, 0)),
        out_shape=jax.ShapeDtypeStruct((b, s, c), bf16),
        scratch_shapes=[pltpu.VMEM((s + POOL_PAD, c), f32)],
        compiler_params=_cparams(("parallel",)),
        name="pool",
    )(u, w_blockdiag, scale.reshape(1, c))


LOG2E = math.log2(math.e)
ONES_ROWS = 16
ACC_ROWS = HEAD_COLS + ONES_ROWS


def _with_ones(vt):
    return jnp.concatenate([vt, jnp.ones((ONES_ROWS, vt.shape[1]), vt.dtype)], axis=0)


class _Chain(NamedTuple):
    qk: Callable
    vt: Callable
    bias: Callable
    st_ref: Any
    acc_ref: Any


def _own_masks(mask_ref, nq):
    t = KEY_BLOCK
    key = lax.broadcasted_iota(jnp.int32, (t, nq), 0)
    qry = lax.broadcasted_iota(jnp.int32, (t, nq), 1)
    second = (qry & t) != 0
    causal = key <= (qry & (t - 1))
    mask_ref[0] = jnp.where(jnp.logical_or(second, causal), 0.0, NEG)
    mask_ref[1] = jnp.where(jnp.logical_and(second, causal), 0.0, NEG)


def _flash_step(st, vt, bias, m_prev, acc_ref):
    col_max = jnp.max(st, axis=0, keepdims=True)
    if bias is None:
        m_new = jnp.maximum(m_prev, col_max)
        shift = m_new
    else:
        m_new = jnp.maximum(m_prev, col_max + bias)
        shift = m_new - bias
    alpha = jnp.exp2(m_prev - m_new)
    p = jnp.exp2(st - shift).astype(bf16)
    acc_ref[...] = alpha * acc_ref[...] + jnp.dot(_with_ones(vt), p, preferred_element_type=f32)
    return m_new


def _flash_tile(qi, chains, mask_ref, nq):
    own0 = 2 * qi
    ms = [jnp.full((1, nq), NEG, f32) for _ in chains]
    for c in chains:
        c.acc_ref[...] = jnp.zeros_like(c.acc_ref)
        c.st_ref[0] = c.qk(own0 + 1)
    for c in chains:
        c.st_ref[1] = c.qk(own0)
    ms = [_flash_step(c.st_ref[0] + mask_ref[1], c.vt(own0 + 1), c.bias(own0 + 1), m, c.acc_ref)
          for c, m in zip(chains, ms)]
    for c in chains:
        c.st_ref[0] = c.qk(0)
    ms = [_flash_step(c.st_ref[1] + mask_ref[0], c.vt(own0), c.bias(own0), m, c.acc_ref)
          for c, m in zip(chains, ms)]

    def body(i, ms):
        j0 = 2 * i
        for c in chains:
            c.st_ref[1] = c.qk(j0 + 1)
        ms = [_flash_step(c.st_ref[0], c.vt(j0), c.bias(j0), m, c.acc_ref) for c, m in zip(chains, ms)]
        for c in chains:
            c.st_ref[0] = c.qk(j0 + 2)
        return tuple(_flash_step(c.st_ref[1], c.vt(j0 + 1), c.bias(j0 + 1), m, c.acc_ref)
                     for c, m in zip(chains, ms))

    lax.fori_loop(0, qi, body, tuple(ms))


def _diff_kernel(lam_init, q_ref, k_ref, vt_ref, lam_ref, g_ref, o_ref, qq_ref, mask_ref, st_ref, acc_ref):
    t = KEY_BLOCK
    tq = QUERY_TILE
    nq = 2 * tq
    nh = st_ref.shape[0]
    _own_masks(mask_ref, nq)
    lane = lax.broadcasted_iota(jnp.int32, (tq, HEAD_COLS), 1)
    lp = lam_ref[...]
    lam = (jnp.exp(jnp.sum(lp[0:1] * lp[1:2], axis=-1, keepdims=True))
           - jnp.exp(jnp.sum(lp[2:3] * lp[3:4], axis=-1, keepdims=True)) + lam_init)
    chains = [_Chain(
        qk=lambda j, h=h: _dot_nt(k_ref[pl.ds(pl.multiple_of(j * t, t), t), _head_cols(h)], qq_ref[h]),
        vt=lambda j, h=h: vt_ref[j, _head_cols(h), :],
        bias=lambda j: None,
        st_ref=st_ref.at[h], acc_ref=acc_ref.at[h]) for h in range(nh)]

    def tile(qi, carry):
        rows = pl.ds(pl.multiple_of(qi * tq, tq), tq)
        for h in range(nh):
            q = q_ref[rows, _head_cols(h)]
            zero = jnp.zeros_like(q)
            qq_ref[h, 0:tq, :] = jnp.where(lane < DIFF_HEAD_DIM, q, zero)
            qq_ref[h, tq:, :] = jnp.where(lane >= DIFF_HEAD_DIM, q, zero)
        _flash_tile(qi, chains, mask_ref, nq)
        for h, c in enumerate(chains):
            ot = c.acc_ref[0:HEAD_COLS, :] / c.acc_ref[HEAD_COLS:HEAD_COLS + 1, :]
            o = (ot[:, :tq] - lam * ot[:, tq:]).T
            o_ref[rows, _head_cols(h)] = (_rms(o, g_ref[...]) * (1.0 - lam_init)).astype(o_ref.dtype)
        return carry

    lax.fori_loop(0, q_ref.shape[0] // tq, tile, 0)


def _head_cols(h):
    return slice(h * HEAD_COLS, (h + 1) * HEAD_COLS)


def _attn_specs(b, s, w, heads_per_step):
    hw = heads_per_step * HEAD_COLS
    assert s % QUERY_TILE == 0 and w % hw == 0
    seq_spec = pl.BlockSpec((None, s, hw), lambda bi, h: (bi, 0, h))
    vt_spec = pl.BlockSpec((None, s // KEY_BLOCK, hw, KEY_BLOCK), lambda bi, h: (bi, 0, h, 0))
    return (b, w // hw), seq_spec, vt_spec


def _diff_attention(q, k, vt, lam_params, subln_g, lam_init):
    b, s, w = q.shape
    nh = DIFF_HEADS_PER_STEP
    grid, seq_spec, vt_spec = _attn_specs(b, s, w, nh)
    nq = 2 * QUERY_TILE
    return pl.pallas_call(
        functools.partial(_diff_kernel, lam_init),
        grid=grid,
        in_specs=[seq_spec, seq_spec, vt_spec, _resident(lam_params.shape), _resident((1, HEAD_COLS))],
        out_specs=seq_spec,
        out_shape=jax.ShapeDtypeStruct((b, s, w), bf16),
        scratch_shapes=[pltpu.VMEM((nh, nq, HEAD_COLS), bf16),
                        pltpu.VMEM((2, KEY_BLOCK, nq), f32),
                        pltpu.VMEM((nh, 2, KEY_BLOCK, nq), f32),
                        pltpu.VMEM((nh, ACC_ROWS, nq), f32)],
        compiler_params=_cparams(("parallel", "parallel")),
        name="diff_attn",
    )(q, k, vt, lam_params, subln_g.reshape(1, HEAD_COLS))


def _moba_kernel(q_ref, k_ref, vt_ref, o_ref, kmean_ref, bias_ref, mask_ref, st_ref, acc_ref):
    t = MOBA_BLOCK
    tq = QUERY_TILE
    nblk = k_ref.shape[0] // t
    nh = st_ref.shape[0]
    nrow = bias_ref.shape[1]
    _own_masks(mask_ref, tq)
    for h in range(nh):
        means = [jnp.mean(k_ref[n * t:(n + 1) * t, _head_cols(h)].astype(f32), axis=0, keepdims=True)
                 for n in range(nblk)]
        if nrow > nblk:
            means.append(jnp.zeros((nrow - nblk, HEAD_COLS), f32))
        km = jnp.concatenate(means, axis=0)
        hi = km.astype(bf16)
        kmean_ref[h, 0:nrow, :] = hi
        kmean_ref[h, nrow:, :] = (km - hi.astype(f32)).astype(bf16)
    blk_i = lax.broadcasted_iota(jnp.int32, (nrow, tq), 0)
    second = lax.broadcasted_iota(jnp.int32, (nrow, tq), 1) >= t
    blk = blk_i.astype(f32)

    def tile(qi, carry):
        own0 = 2 * qi
        rows = pl.ds(pl.multiple_of(qi * tq, tq), tq)
        valid = blk_i < jnp.where(second, own0 + 1, own0)
        for h in range(nh):
            parts = _dot_nt(kmean_ref[h], q_ref[rows, _head_cols(h)])
            gate = parts[0:nrow] + parts[nrow:]
            g = jnp.where(valid, gate, NEG)
            sel = jnp.zeros((nrow, tq), jnp.bool_)
            for _ in range(MOBA_TOPK):
                mx = jnp.max(g, axis=0, keepdims=True)
                first = jnp.min(jnp.where(g == mx, blk, float(nrow)), axis=0, keepdims=True)
                pick = blk == first
                sel = jnp.logical_or(sel, jnp.logical_and(pick, valid))
                g = jnp.where(pick, -jnp.inf, g)
            own_free = jnp.logical_or(blk_i == own0 + 1,
                                      jnp.logical_and(blk_i == own0, jnp.logical_not(second)))
            bias_ref[h] = jnp.where(jnp.logical_or(sel, own_free), 0.0, NEG)
        chains = [_Chain(
            qk=lambda j, h=h: _dot_nt(k_ref[pl.ds(pl.multiple_of(j * t, t), t), _head_cols(h)],
                                      q_ref[rows, _head_cols(h)]),
            vt=lambda j, h=h: vt_ref[j, _head_cols(h), :],
            bias=lambda j, h=h: bias_ref[h, pl.ds(j, 1), :],
            st_ref=st_ref.at[h], acc_ref=acc_ref.at[h]) for h in range(nh)]
        _flash_tile(qi, chains, mask_ref, tq)
        for h, c in enumerate(chains):
            o_ref[rows, _head_cols(h)] = (
                c.acc_ref[0:HEAD_COLS, :] / c.acc_ref[HEAD_COLS:HEAD_COLS + 1, :]).T.astype(o_ref.dtype)
        return carry

    lax.fori_loop(0, q_ref.shape[0] // tq, tile, 0)


def _moba_attention(q, k, vt):
    b, s, w = q.shape
    assert MOBA_BLOCK == KEY_BLOCK and QUERY_TILE == 2 * MOBA_BLOCK
    nrow = -(-(s // MOBA_BLOCK) // SUBLANES) * SUBLANES
    nh = MOBA_HEADS_PER_STEP
    grid, seq_spec, vt_spec = _attn_specs(b, s, w, nh)
    return pl.pallas_call(
        _moba_kernel,
        grid=grid,
        in_specs=[seq_spec, seq_spec, vt_spec],
        out_specs=seq_spec,
        out_shape=jax.ShapeDtypeStruct((b, s, w), bf16),
        scratch_shapes=[pltpu.VMEM((nh, 2 * nrow, HEAD_COLS), bf16),
                        pltpu.VMEM((nh, nrow, QUERY_TILE), f32),
                        pltpu.VMEM((2, KEY_BLOCK, QUERY_TILE), f32),
                        pltpu.VMEM((nh, 2, KEY_BLOCK, QUERY_TILE), f32),
                        pltpu.VMEM((nh, ACC_ROWS, QUERY_TILE), f32)],
        compiler_params=_cparams(("parallel", "parallel")),
        name="moba_attn",
    )(q, k, vt)


CONV_PAD = 32
CONV_HALO = (CONV_WIDTH - 1) // SUBLANES * SUBLANES


def _conv_kernel(u_ref, w_ref, b_ref, g_ref, beta_ref, o_ref, pad_ref, sh_ref):
    s = u_ref.shape[0]
    pad_ref[0:CONV_PAD, :] = jnp.zeros((CONV_PAD, CONV_CH), f32)
    for c in range(s // CONV_CHUNK):
        rows = pl.ds(c * CONV_CHUNK, CONV_CHUNK)
        u = u_ref[rows, :]
        pad_ref[pl.ds(CONV_PAD + c * CONV_CHUNK, CONV_CHUNK), :] = (
            u[:, :CONV_CH] * jax.nn.sigmoid(u[:, CONV_CH:]))
    w = w_ref[...]
    for c in range(s // CONV_CHUNK):
        base = CONV_PAD + c * CONV_CHUNK - CONV_HALO
        for r in range(SUBLANES):
            sh_ref[r] = pad_ref[pl.ds(base - r, CONV_CHUNK + CONV_HALO), :]
        acc = jnp.zeros((CONV_CHUNK, CONV_CH), f32)
        for shift in range(CONV_WIDTH):
            a, r = divmod(shift, SUBLANES)
            j = CONV_WIDTH - 1 - shift
            acc = acc + sh_ref[r, pl.ds(CONV_HALO - SUBLANES * a, CONV_CHUNK), :] * w[j:j + 1, :]
        h = acc + b_ref[...]
        mu = jnp.mean(h, axis=-1, keepdims=True)
        var = jnp.mean(jnp.square(h - mu), axis=-1, keepdims=True)
        y = (h - mu) * lax.rsqrt(var + EPS) * g_ref[...] + beta_ref[...]
        o_ref[pl.ds(c * CONV_CHUNK, CONV_CHUNK), :] = (y * jax.nn.sigmoid(y)).astype(o_ref.dtype)


def _conformer_conv(u, conv_w, conv_b, ln_g, ln_b):
    b, s, c2 = u.shape
    c = c2 // 2
    vec = _resident((1, c))
    return pl.pallas_call(
        _conv_kernel,
        grid=(b,),
        in_specs=[pl.BlockSpec((None, s, c2), lambda i: (i, 0, 0)),
                  _resident(conv_w.shape), vec, vec, vec],
        out_specs=pl.BlockSpec((None, s, c), lambda i: (i, 0, 0)),
        out_shape=jax.ShapeDtypeStruct((b, s, c), bf16),
        scratch_shapes=[pltpu.VMEM((s + CONV_PAD, c), f32),
                        pltpu.VMEM((SUBLANES, CONV_CHUNK + CONV_HALO, c), f32)],
        compiler_params=_cparams(("parallel",)),
        name="conformer_conv",
    )(u, conv_w, conv_b.reshape(1, c), ln_g.reshape(1, c), ln_b.reshape(1, c))


def _mix_ffn_kernel(final, h_ref, ya_ref, yb_ref, wo_ref, g_ref, wg_ref, wu_ref, wd_ref,
                    gf_ref, o_ref, act_ref):
    ca = ya_ref.shape[1]
    h1 = (h_ref[...]
          + jnp.dot(ya_ref[...], wo_ref[0:ca, :], preferred_element_type=f32)
          + jnp.dot(yb_ref[...], wo_ref[ca:, :], preferred_element_type=f32))
    hn = _rms(h1, g_ref[...]).astype(bf16)
    for a, b in FFN_CHUNKS:
        gate = jnp.dot(hn, wg_ref[:, a:b], preferred_element_type=f32)
        up = jnp.dot(hn, wu_ref[:, a:b], preferred_element_type=f32)
        act_ref[:, a:b] = (gate * jax.nn.sigmoid(gate) * up).astype(bf16)
    h2 = h1 + jnp.dot(act_ref[...], wd_ref[...], preferred_element_type=f32)
    if final:
        h2 = _rms(h2, gf_ref[...])
    o_ref[...] = h2


def _mix_ffn(h, ya, yb, w_out, j, g, wg, wu, wd, layer, g_final, final):
    n, d = h.shape
    ca, cb = ya.shape[1], yb.shape[1]
    dff = wg.shape[2]
    assert FFN_CHUNKS[-1][1] == dff and w_out.shape[1] == ca + cb
    row = lambda c: pl.BlockSpec((ROW_TILE, c), lambda i: (i, 0))
    return pl.pallas_call(
        functools.partial(_mix_ffn_kernel, final),
        grid=(n // ROW_TILE,),
        in_specs=[row(d), row(ca), row(cb), _resident(w_out.shape, j),
                  _resident((1, d)), _resident(wg.shape, layer), _resident(wu.shape, layer),
                  _resident(wd.shape, layer), _resident((1, d))],
        out_specs=row(d),
        out_shape=jax.ShapeDtypeStruct((n, d), f32),
        scratch_shapes=[pltpu.VMEM((ROW_TILE, dff), bf16)],
        compiler_params=_cparams(("parallel",)),
        name="mix_ffn",
    )(h, ya, yb, w_out, g.reshape(1, d), wg, wu, wd, g_final.reshape(1, d))


def _block_diag(w):
    g, c, _ = w.shape
    eye = jnp.eye(g, dtype=w.dtype)
    return (eye[:, None, :, None] * w[:, :, None, :]).reshape(g * c, g * c)


def kernel(x, norm_mix_g, norm_ffn_g, norm_final_g, ab_w_in, ab_w_out, pool_w, pool_scale, diff_lambda, diff_subln_g, cd_w_in, cd_w_out, conv_w, conv_b, conv_ln_g, conv_ln_b, ffn_w_gate, ffn_w_up, ffn_w_down):
    bsz, s, d = x.shape
    n = bsz * s
    depth = norm_mix_g.shape[0]
    assert s % ROW_TILE == 0 and s % CONV_CHUNK == 0 and s % POOL_CHUNK == 0
    h = x.reshape(n, d)
    ab_vcol = POOL_WIDTH + 2 * DIFF_QK
    cd_vcol = 2 * CONV_CH + 2 * MOBA_WIDTH
    ab_w, ab_wvt = ab_w_in.astype(bf16), ab_w_in[:, :, ab_vcol:].astype(bf16).transpose(0, 2, 1)
    cd_w, cd_wvt = cd_w_in.astype(bf16), cd_w_in[:, :, cd_vcol:].astype(bf16).transpose(0, 2, 1)
    ab_wo, cd_wo = ab_w_out.astype(bf16), cd_w_out.astype(bf16)
    wg, wu, wd = ffn_w_gate.astype(bf16), ffn_w_up.astype(bf16), ffn_w_down.astype(bf16)
    for layer in range(depth):
        j = layer // 2
        if layer % 2 == 0:
            o = POOL_WIDTH
            groups = ((0, o, 1.0, f32),
                      (o, o + DIFF_QK, DIFF_HEAD_DIM ** -0.5 * LOG2E, bf16),
                      (o + DIFF_QK, ab_vcol, 1.0, bf16))
            u, q, k, vt = _norm_proj(h, norm_mix_g[layer], ab_w, ab_wvt, j, groups, s)
            lam_init = 0.8 - 0.6 * math.exp(-0.3 * layer)
            ya = _pool(u.reshape(bsz, s, -1), _block_diag(pool_w[j]).astype(bf16), pool_scale[j])
            yb = _diff_attention(q.reshape(bsz, s, -1), k.reshape(bsz, s, -1), vt,
                                 diff_lambda[j], diff_subln_g[j], lam_init)
            w_out = ab_wo
        else:
            o = 2 * CONV_CH
            groups = ((0, o, 1.0, f32),
                      (o, o + MOBA_WIDTH, MOBA_HEAD_DIM ** -0.5 * LOG2E, bf16),
                      (o + MOBA_WIDTH, cd_vcol, 1.0, bf16))
            u, q, k, vt = _norm_proj(h, norm_mix_g[layer], cd_w, cd_wvt, j, groups, s)
            ya = _conformer_conv(u.reshape(bsz, s, -1), conv_w[j], conv_b[j], conv_ln_g[j], conv_ln_b[j])
            yb = _moba_attention(q.reshape(bsz, s, -1), k.reshape(bsz, s, -1), vt)
            w_out = cd_wo
        h = _mix_ffn(h, ya.reshape(n, -1), yb.reshape(n, -1), w_out, j, norm_ffn_g[layer],
                     wg, wu, wd, layer, norm_final_g, layer == depth - 1)
    return h.reshape(bsz, s, d)
```

```python
import functools
import math
from typing import Any, Callable, NamedTuple

import jax
import jax.numpy as jnp
from jax import lax
from jax.experimental import pallas as pl
from jax.experimental.pallas import tpu as pltpu

EPS = 1e-6
NEG = -1e30

POOL_GROUP_DIM = 64
POOL_WIDTH = 256
DIFF_HEAD_DIM = 64
DIFF_QK = 768
CONV_CH = 256
CONV_WIDTH = 31
MOBA_HEAD_DIM = 128
MOBA_WIDTH = 768
MOBA_BLOCK = 256
MOBA_TOPK = 3

LANES = 128
SUBLANES = 8
HEAD_COLS = 128
VMEM_LIMIT = 56 * 1024 * 1024

ROW_TILE = 512
KEY_BLOCK = 256
QUERY_TILE = 2 * KEY_BLOCK
DIFF_HEADS_PER_STEP = 2
MOBA_HEADS_PER_STEP = 3
CONV_CHUNK = 512
POOL_CHUNK = 512
FFN_CHUNKS = ((0, 768), (768, 1536), (1536, 2304), (2304, 2816))

bf16 = jnp.bfloat16
f32 = jnp.float32


def _cparams(sem):
    return pltpu.CompilerParams(dimension_semantics=sem, vmem_limit_bytes=VMEM_LIMIT)


def _resident(shape, layer=None):
    nd = len(shape)
    if layer is None:
        return pl.BlockSpec(shape, lambda *_: (0,) * nd, pipeline_mode=pl.Buffered(1))
    return pl.BlockSpec((None,) + tuple(shape[1:]), lambda *_: (layer,) + (0,) * (nd - 1),
                        pipeline_mode=pl.Buffered(1))


def _rms(x, g):
    ms = jnp.mean(x * x, axis=-1, keepdims=True)
    return x * lax.rsqrt(ms + EPS) * g


def _dot_nt(a, b):
    return lax.dot_general(a, b, (((1,), (1,)), ((), ())), preferred_element_type=f32)


def _norm_proj_kernel(groups, x_ref, g_ref, w_ref, wvt_ref, *out_refs):
    _project(groups, _rms(x_ref[...], g_ref[...]).astype(bf16), w_ref, wvt_ref, out_refs)


def _project(groups, xn, w_ref, wvt_ref, out_refs):
    for (a, b, scale, _), o_ref in zip(groups, out_refs[:-1]):
        y = jnp.dot(xn, w_ref[:, a:b], preferred_element_type=f32)
        if scale != 1.0:
            y = y * scale
        o_ref[...] = y.astype(o_ref.dtype)
    vt_ref = out_refs[-1]
    vt = _dot_nt(wvt_ref[...], xn)
    for c in range(ROW_TILE // KEY_BLOCK):
        vt_ref[c] = vt[:, c * KEY_BLOCK:(c + 1) * KEY_BLOCK].astype(vt_ref.dtype)


def _norm_proj(h, g, w, wvt, j, groups, seq):
    n, d = h.shape
    out_specs, out_shape = _proj_outputs(n, wvt.shape[1], groups, seq)
    return pl.pallas_call(
        functools.partial(_norm_proj_kernel, groups),
        grid=(n // ROW_TILE,),
        in_specs=[pl.BlockSpec((ROW_TILE, d), lambda i: (i, 0)),
                  _resident((1, d)),
                  _resident(w.shape, j),
                  _resident(wvt.shape, j)],
        out_specs=out_specs,
        out_shape=out_shape,
        compiler_params=_cparams(("parallel",)),
        name="norm_proj",
    )(h, g.reshape(1, d), w, wvt)


def _proj_outputs(n, vw, groups, seq):
    tiles_per_seq = seq // ROW_TILE
    blocks_per_tile = ROW_TILE // KEY_BLOCK
    out_specs = [pl.BlockSpec((ROW_TILE, b - a), lambda i: (i, 0)) for a, b, _, _ in groups] + [
        pl.BlockSpec((None, blocks_per_tile, vw, KEY_BLOCK),
                     lambda i: (i // tiles_per_seq, i % tiles_per_seq, 0, 0))]
    out_shape = [jax.ShapeDtypeStruct((n, b - a), dt) for a, b, _, dt in groups] + [
        jax.ShapeDtypeStruct((n // seq, seq // KEY_BLOCK, vw, KEY_BLOCK), bf16)]
    return out_specs, out_shape


POOL_PAD = 16


def _pool_kernel(u_ref, w_ref, scale_ref, o_ref, pad_ref):
    s = u_ref.shape[0]
    pad_ref[0:POOL_PAD, :] = jnp.zeros((POOL_PAD, POOL_WIDTH), f32)
    pad_ref[POOL_PAD:, :] = u_ref[...]
    lane = lax.broadcasted_iota(jnp.int32, (POOL_CHUNK, POOL_WIDTH), 1)
    g0, g1, g2 = (lane < POOL_GROUP_DIM * (i + 1) for i in range(3))
    for c in range(s // POOL_CHUNK):
        base = c * POOL_CHUNK

        def win(k):
            return pad_ref[pl.ds(POOL_PAD + base - k, POOL_CHUNK), :]

        u = win(0)
        pos = base + lax.broadcasted_iota(jnp.int32, (POOL_CHUNK, POOL_WIDTH), 0)
        s2 = u + win(1)
        s4 = s2 + win(2) + win(3)
        s8 = s4 + win(4) + win(5) + win(6) + win(7)
        s16 = s8
        for k in range(8, 16):
            s16 = s16 + win(k)
        tot = jnp.where(g0, s2, jnp.where(g1, s4, jnp.where(g2, s8, s16)))
        wlen = jnp.where(g0, 2, jnp.where(g1, 4, jnp.where(g2, 8, 16)))
        cnt = jnp.minimum(pos + 1, wlen).astype(f32)
        d = (tot / cnt - u).astype(bf16)
        y = jnp.dot(d, w_ref[...], preferred_element_type=f32) * scale_ref[...]
        o_ref[pl.ds(base, POOL_CHUNK), :] = y.astype(o_ref.dtype)


def _pool(u, w_blockdiag, scale):
    b, s, c = u.shape
    return pl.pallas_call(
        _pool_kernel,
        grid=(b,),
        in_specs=[pl.BlockSpec((None, s, c), lambda i: (i, 0, 0)),
                  _resident((c, c)),
                  _resident((1, c))],
        out_specs=pl.BlockSpec((None, s, c), lambda i: (i, 0, 0)),
        out_shape=jax.ShapeDtypeStruct((b, s, c), bf16),
        scratch_shapes=[pltpu.VMEM((s + POOL_PAD, c), f32)],
        compiler_params=_cparams(("parallel",)),
        name="pool",
    )(u, w_blockdiag, scale.reshape(1, c))


LOG2E = math.log2(math.e)
ONES_ROWS = 16
ACC_ROWS = HEAD_COLS + ONES_ROWS


def _with_ones(vt):
    return jnp.concatenate([vt, jnp.ones((ONES_ROWS, vt.shape[1]), vt.dtype)], axis=0)


class _Chain(NamedTuple):
    qk: Callable
    vt: Callable
    bias: Callable
    st_ref: Any
    acc_ref: Any


def _own_masks(mask_ref, nq):
    t = KEY_BLOCK
    key = lax.broadcasted_iota(jnp.int32, (t, nq), 0)
    qry = lax.broadcasted_iota(jnp.int32, (t, nq), 1)
    second = (qry & t) != 0
    causal = key <= (qry & (t - 1))
    mask_ref[0] = jnp.where(jnp.logical_or(second, causal), 0.0, NEG)
    mask_ref[1] = jnp.where(jnp.logical_and(second, causal), 0.0, NEG)


def _flash_step(st, vt, bias, m_prev, acc_ref):
    col_max = jnp.max(st, axis=0, keepdims=True)
    if bias is None:
        m_new = jnp.maximum(m_prev, col_max)
        shift = m_new
    else:
        m_new = jnp.maximum(m_prev, col_max + bias)
        shift = m_new - bias
    alpha = jnp.exp2(m_prev - m_new)
    p = jnp.exp2(st - shift).astype(bf16)
    acc_ref[...] = alpha * acc_ref[...] + jnp.dot(_with_ones(vt), p, preferred_element_type=f32)
    return m_new


def _flash_tile(qi, chains, mask_ref, nq):
    own0 = 2 * qi
    ms = [jnp.full((1, nq), NEG, f32) for _ in chains]
    for c in chains:
        c.acc_ref[...] = jnp.zeros_like(c.acc_ref)
        c.st_ref[0] = c.qk(own0 + 1)
    for c in chains:
        c.st_ref[1] = c.qk(own0)
    ms = [_flash_step(c.st_ref[0] + mask_ref[1], c.vt(own0 + 1), c.bias(own0 + 1), m, c.acc_ref)
          for c, m in zip(chains, ms)]
    for c in chains:
        c.st_ref[0] = c.qk(0)
    ms = [_flash_step(c.st_ref[1] + mask_ref[0], c.vt(own0), c.bias(own0), m, c.acc_ref)
          for c, m in zip(chains, ms)]

    def body(i, ms):
        j0 = 2 * i
        for c in chains:
            c.st_ref[1] = c.qk(j0 + 1)
        ms = [_flash_step(c.st_ref[0], c.vt(j0), c.bias(j0), m, c.acc_ref) for c, m in zip(chains, ms)]
        for c in chains:
            c.st_ref[0] = c.qk(j0 + 2)
        return tuple(_flash_step(c.st_ref[1], c.vt(j0 + 1), c.bias(j0 + 1), m, c.acc_ref)
                     for c, m in zip(chains, ms))

    lax.fori_loop(0, qi, body, tuple(ms))


def _diff_kernel(lam_init, q_ref, k_ref, vt_ref, lam_ref, g_ref, o_ref, qq_ref, mask_ref, st_ref, acc_ref):
    t = KEY_BLOCK
    tq = QUERY_TILE
    nq = 2 * tq
    nh = st_ref.shape[0]
    _own_masks(mask_ref, nq)
    lane = lax.broadcasted_iota(jnp.int32, (tq, HEAD_COLS), 1)
    lp = lam_ref[...]
    lam = (jnp.exp(jnp.sum(lp[0:1] * lp[1:2], axis=-1, keepdims=True))
           - jnp.exp(jnp.sum(lp[2:3] * lp[3:4], axis=-1, keepdims=True)) + lam_init)
    chains = [_Chain(
        qk=lambda j, h=h: _dot_nt(k_ref[pl.ds(pl.multiple_of(j * t, t), t), _head_cols(h)], qq_ref[h]),
        vt=lambda j, h=h: vt_ref[j, _head_cols(h), :],
        bias=lambda j: None,
        st_ref=st_ref.at[h], acc_ref=acc_ref.at[h]) for h in range(nh)]

    def tile(qi, carry):
        rows = pl.ds(pl.multiple_of(qi * tq, tq), tq)
        for h in range(nh):
            q = q_ref[rows, _head_cols(h)]
            zero = jnp.zeros_like(q)
            qq_ref[h, 0:tq, :] = jnp.where(lane < DIFF_HEAD_DIM, q, zero)
            qq_ref[h, tq:, :] = jnp.where(lane >= DIFF_HEAD_DIM, q, zero)
        _flash_tile(qi, chains, mask_ref, nq)
        for h, c in enumerate(chains):
            ot = c.acc_ref[0:HEAD_COLS, :] / c.acc_ref[HEAD_COLS:HEAD_COLS + 1, :]
            o = (ot[:, :tq] - lam * ot[:, tq:]).T
            o_ref[rows, _head_cols(h)] = (_rms(o, g_ref[...]) * (1.0 - lam_init)).astype(o_ref.dtype)
        return carry

    lax.fori_loop(0, q_ref.shape[0] // tq, tile, 0)


def _head_cols(h):
    return slice(h * HEAD_COLS, (h + 1) * HEAD_COLS)


def _attn_specs(b, s, w, heads_per_step):
    hw = heads_per_step * HEAD_COLS
    assert s % QUERY_TILE == 0 and w % hw == 0
    seq_spec = pl.BlockSpec((None, s, hw), lambda bi, h: (bi, 0, h))
    vt_spec = pl.BlockSpec((None, s // KEY_BLOCK, hw, KEY_BLOCK), lambda bi, h: (bi, 0, h, 0))
    return (b, w // hw), seq_spec, vt_spec


def _diff_attention(q, k, vt, lam_params, subln_g, lam_init):
    b, s, w = q.shape
    nh = DIFF_HEADS_PER_STEP
    grid, seq_spec, vt_spec = _attn_specs(b, s, w, nh)
    nq = 2 * QUERY_TILE
    return pl.pallas_call(
        functools.partial(_diff_kernel, lam_init),
        grid=grid,
        in_specs=[seq_spec, seq_spec, vt_spec, _resident(lam_params.shape), _resident((1, HEAD_COLS))],
        out_specs=seq_spec,
        out_shape=jax.ShapeDtypeStruct((b, s, w), bf16),
        scratch_shapes=[pltpu.VMEM((nh, nq, HEAD_COLS), bf16),
                        pltpu.VMEM((2, KEY_BLOCK, nq), f32),
                        pltpu.VMEM((nh, 2, KEY_BLOCK, nq), f32),
                        pltpu.VMEM((nh, ACC_ROWS, nq), f32)],
        compiler_params=_cparams(("parallel", "parallel")),
        name="diff_attn",
    )(q, k, vt, lam_params, subln_g.reshape(1, HEAD_COLS))


def _moba_kernel(q_ref, k_ref, vt_ref, o_ref, kmean_ref, bias_ref, mask_ref, st_ref, acc_ref):
    t = MOBA_BLOCK
    tq = QUERY_TILE
    nblk = k_ref.shape[0] // t
    nh = st_ref.shape[0]
    nrow = bias_ref.shape[1]
    _own_masks(mask_ref, tq)
    for h in range(nh):
        means = [jnp.mean(k_ref[n * t:(n + 1) * t, _head_cols(h)].astype(f32), axis=0, keepdims=True)
                 for n in range(nblk)]
        if nrow > nblk:
            means.append(jnp.zeros((nrow - nblk, HEAD_COLS), f32))
        km = jnp.concatenate(means, axis=0)
        hi = km.astype(bf16)
        kmean_ref[h, 0:nrow, :] = hi
        kmean_ref[h, nrow:, :] = (km - hi.astype(f32)).astype(bf16)
    blk_i = lax.broadcasted_iota(jnp.int32, (nrow, tq), 0)
    second = lax.broadcasted_iota(jnp.int32, (nrow, tq), 1) >= t
    blk = blk_i.astype(f32)

    def tile(qi, carry):
        own0 = 2 * qi
        rows = pl.ds(pl.multiple_of(qi * tq, tq), tq)
        valid = blk_i < jnp.where(second, own0 + 1, own0)
        for h in range(nh):
            parts = _dot_nt(kmean_ref[h], q_ref[rows, _head_cols(h)])
            gate = parts[0:nrow] + parts[nrow:]
            g = jnp.where(valid, gate, NEG)
            sel = jnp.zeros((nrow, tq), jnp.bool_)
            for _ in range(MOBA_TOPK):
                mx = jnp.max(g, axis=0, keepdims=True)
                first = jnp.min(jnp.where(g == mx, blk, float(nrow)), axis=0, keepdims=True)
                pick = blk == first
                sel = jnp.logical_or(sel, jnp.logical_and(pick, valid))
                g = jnp.where(pick, -jnp.inf, g)
            own_free = jnp.logical_or(blk_i == own0 + 1,
                                      jnp.logical_and(blk_i == own0, jnp.logical_not(second)))
            bias_ref[h] = jnp.where(jnp.logical_or(sel, own_free), 0.0, NEG)
        chains = [_Chain(
            qk=lambda j, h=h: _dot_nt(k_ref[pl.ds(pl.multiple_of(j * t, t), t), _head_cols(h)],
                                      q_ref[rows, _head_cols(h)]),
            vt=lambda j, h=h: vt_ref[j, _head_cols(h), :],
            bias=lambda j, h=h: bias_ref[h, pl.ds(j, 1), :],
            st_ref=st_ref.at[h], acc_ref=acc_ref.at[h]) for h in range(nh)]
        _flash_tile(qi, chains, mask_ref, tq)
        for h, c in enumerate(chains):
            o_ref[rows, _head_cols(h)] = (
                c.acc_ref[0:HEAD_COLS, :] / c.acc_ref[HEAD_COLS:HEAD_COLS + 1, :]).T.astype(o_ref.dtype)
        return carry

    lax.fori_loop(0, q_ref.shape[0] // tq, tile, 0)


def _moba_attention(q, k, vt):
    b, s, w = q.shape
    assert MOBA_BLOCK == KEY_BLOCK and QUERY_TILE == 2 * MOBA_BLOCK
    nrow = -(-(s // MOBA_BLOCK) // SUBLANES) * SUBLANES
    nh = MOBA_HEADS_PER_STEP
    grid, seq_spec, vt_spec = _attn_specs(b, s, w, nh)
    return pl.pallas_call(
        _moba_kernel,
        grid=grid,
        in_specs=[seq_spec, seq_spec, vt_spec],
        out_specs=seq_spec,
        out_shape=jax.ShapeDtypeStruct((b, s, w), bf16),
        scratch_shapes=[pltpu.VMEM((nh, 2 * nrow, HEAD_COLS), bf16),
                        pltpu.VMEM((nh, nrow, QUERY_TILE), f32),
                        pltpu.VMEM((2, KEY_BLOCK, QUERY_TILE), f32),
                        pltpu.VMEM((nh, 2, KEY_BLOCK, QUERY_TILE), f32),
                        pltpu.VMEM((nh, ACC_ROWS, QUERY_TILE), f32)],
        compiler_params=_cparams(("parallel", "parallel")),
        name="moba_attn",
    )(q, k, vt)


CONV_PAD = 32
CONV_HALO = (CONV_WIDTH - 1) // SUBLANES * SUBLANES


def _conv_kernel(u_ref, w_ref, b_ref, g_ref, beta_ref, o_ref, pad_ref, sh_ref):
    s = u_ref.shape[0]
    pad_ref[0:CONV_PAD, :] = jnp.zeros((CONV_PAD, CONV_CH), f32)
    for c in range(s // CONV_CHUNK):
        rows = pl.ds(c * CONV_CHUNK, CONV_CHUNK)
        u = u_ref[rows, :]
        pad_ref[pl.ds(CONV_PAD + c * CONV_CHUNK, CONV_CHUNK), :] = (
            u[:, :CONV_CH] * jax.nn.sigmoid(u[:, CONV_CH:]))
    w = w_ref[...]
    for c in range(s // CONV_CHUNK):
        base = CONV_PAD + c * CONV_CHUNK - CONV_HALO
        for r in range(SUBLANES):
            sh_ref[r] = pad_ref[pl.ds(base - r, CONV_CHUNK + CONV_HALO), :]
        acc = jnp.zeros((CONV_CHUNK, CONV_CH), f32)
        for shift in range(CONV_WIDTH):
            a, r = divmod(shift, SUBLANES)
            j = CONV_WIDTH - 1 - shift
            acc = acc + sh_ref[r, pl.ds(CONV_HALO - SUBLANES * a, CONV_CHUNK), :] * w[j:j + 1, :]
        h = acc + b_ref[...]
        mu = jnp.mean(h, axis=-1, keepdims=True)
        var = jnp.mean(jnp.square(h - mu), axis=-1, keepdims=True)
        y = (h - mu) * lax.rsqrt(var + EPS) * g_ref[...] + beta_ref[...]
        o_ref[pl.ds(c * CONV_CHUNK, CONV_CHUNK), :] = (y * jax.nn.sigmoid(y)).astype(o_ref.dtype)


def _conformer_conv(u, conv_w, conv_b, ln_g, ln_b):
    b, s, c2 = u.shape
    c = c2 // 2
    vec = _resident((1, c))
    return pl.pallas_call(
        _conv_kernel,
        grid=(b,),
        in_specs=[pl.BlockSpec((None, s, c2), lambda i: (i, 0, 0)),
                  _resident(conv_w.shape), vec, vec, vec],
        out_specs=pl.BlockSpec((None, s, c), lambda i: (i, 0, 0)),
        out_shape=jax.ShapeDtypeStruct((b, s, c), bf16),
        scratch_shapes=[pltpu.VMEM((s + CONV_PAD, c), f32),
                        pltpu.VMEM((SUBLANES, CONV_CHUNK + CONV_HALO, c), f32)],
        compiler_params=_cparams(("parallel",)),
        name="conformer_conv",
    )(u, conv_w, conv_b.reshape(1, c), ln_g.reshape(1, c), ln_b.reshape(1, c))


def _mix_ffn_kernel(groups, h_ref, ya_ref, yb_ref, wo_ref, g_ref, wg_ref, wu_ref, wd_ref,
                    gn_ref, *refs):
    final = groups is None
    nw = 0 if final else 2
    o_ref, act_ref = refs[nw], refs[-1]
    ca = ya_ref.shape[1]
    h1 = (h_ref[...]
          + jnp.dot(ya_ref[...], wo_ref[0:ca, :], preferred_element_type=f32)
          + jnp.dot(yb_ref[...], wo_ref[ca:, :], preferred_element_type=f32))
    hn = _rms(h1, g_ref[...]).astype(bf16)
    for a, b in FFN_CHUNKS:
        gate = jnp.dot(hn, wg_ref[:, a:b], preferred_element_type=f32)
        up = jnp.dot(hn, wu_ref[:, a:b], preferred_element_type=f32)
        act_ref[:, a:b] = (gate * jax.nn.sigmoid(gate) * up).astype(bf16)
    h2 = h1 + jnp.dot(act_ref[...], wd_ref[...], preferred_element_type=f32)
    if final:
        o_ref[...] = _rms(h2, gn_ref[...])
    else:
        o_ref[...] = h2
        _project(groups, _rms(h2, gn_ref[...]).astype(bf16), refs[0], refs[1], refs[nw + 1:-1])


def _mix_ffn(h, ya, yb, w_out, j, g, wg, wu, wd, layer, g_next, nxt=None):
    n, d = h.shape
    ca, cb = ya.shape[1], yb.shape[1]
    dff = wg.shape[2]
    assert FFN_CHUNKS[-1][1] == dff and w_out.shape[1] == ca + cb
    row = lambda c: pl.BlockSpec((ROW_TILE, c), lambda i: (i, 0))
    in_specs = [row(d), row(ca), row(cb), _resident(w_out.shape, j),
                _resident((1, d)), _resident(wg.shape, layer), _resident(wu.shape, layer),
                _resident(wd.shape, layer), _resident((1, d))]
    args = [h, ya, yb, w_out, g.reshape(1, d), wg, wu, wd, g_next.reshape(1, d)]
    out_specs, out_shape, groups = [row(d)], [jax.ShapeDtypeStruct((n, d), f32)], None
    if nxt is not None:
        w, wvt, jn, groups, seq = nxt
        in_specs += [_resident(w.shape, jn), _resident(wvt.shape, jn)]
        args += [w, wvt]
        p_specs, p_shape = _proj_outputs(n, wvt.shape[1], groups, seq)
        out_specs, out_shape = out_specs + p_specs, out_shape + p_shape
    outs = pl.pallas_call(
        functools.partial(_mix_ffn_kernel, groups),
        grid=(n // ROW_TILE,),
        in_specs=in_specs,
        out_specs=out_specs,
        out_shape=out_shape,
        scratch_shapes=[pltpu.VMEM((ROW_TILE, dff), bf16)],
        compiler_params=_cparams(("parallel",)),
        name="mix_ffn",
    )(*args)
    return outs[0] if nxt is None else outs


def _block_diag(w):
    g, c, _ = w.shape
    eye = jnp.eye(g, dtype=w.dtype)
    return (eye[:, None, :, None] * w[:, :, None, :]).reshape(g * c, g * c)


def kernel(x, norm_mix_g, norm_ffn_g, norm_final_g, ab_w_in, ab_w_out, pool_w, pool_scale, diff_lambda, diff_subln_g, cd_w_in, cd_w_out, conv_w, conv_b, conv_ln_g, conv_ln_b, ffn_w_gate, ffn_w_up, ffn_w_down):
    bsz, s, d = x.shape
    n = bsz * s
    depth = norm_mix_g.shape[0]
    assert s % ROW_TILE == 0 and s % CONV_CHUNK == 0 and s % POOL_CHUNK == 0
    h = x.reshape(n, d)
    ab_vcol = POOL_WIDTH + 2 * DIFF_QK
    cd_vcol = 2 * CONV_CH + 2 * MOBA_WIDTH
    ab_w, ab_wvt = ab_w_in.astype(bf16), ab_w_in[:, :, ab_vcol:].astype(bf16).transpose(0, 2, 1)
    cd_w, cd_wvt = cd_w_in.astype(bf16), cd_w_in[:, :, cd_vcol:].astype(bf16).transpose(0, 2, 1)
    ab_wo, cd_wo = ab_w_out.astype(bf16), cd_w_out.astype(bf16)
    wg, wu, wd = ffn_w_gate.astype(bf16), ffn_w_up.astype(bf16), ffn_w_down.astype(bf16)
    o_ab, o_cd = POOL_WIDTH, 2 * CONV_CH
    ab_groups = ((0, o_ab, 1.0, f32),
                 (o_ab, o_ab + DIFF_QK, DIFF_HEAD_DIM ** -0.5 * LOG2E, bf16),
                 (o_ab + DIFF_QK, ab_vcol, 1.0, bf16))
    cd_groups = ((0, o_cd, 1.0, f32),
                 (o_cd, o_cd + MOBA_WIDTH, MOBA_HEAD_DIM ** -0.5 * LOG2E, bf16),
                 (o_cd + MOBA_WIDTH, cd_vcol, 1.0, bf16))

    def proj_of(layer):
        return ((ab_w, ab_wvt, layer // 2, ab_groups, s) if layer % 2 == 0
                else (cd_w, cd_wvt, layer // 2, cd_groups, s))

    w0, wvt0, j0, groups0, _ = proj_of(0)
    u, q, k, vt = _norm_proj(h, norm_mix_g[0], w0, wvt0, j0, groups0, s)
    for layer in range(depth):
        j = layer // 2
        if layer % 2 == 0:
            lam_init = 0.8 - 0.6 * math.exp(-0.3 * layer)
            ya = _pool(u.reshape(bsz, s, -1), _block_diag(pool_w[j]).astype(bf16), pool_scale[j])
            yb = _diff_attention(q.reshape(bsz, s, -1), k.reshape(bsz, s, -1), vt,
                                 diff_lambda[j], diff_subln_g[j], lam_init)
            w_out = ab_wo
        else:
            ya = _conformer_conv(u.reshape(bsz, s, -1), conv_w[j], conv_b[j], conv_ln_g[j], conv_ln_b[j])
            yb = _moba_attention(q.reshape(bsz, s, -1), k.reshape(bsz, s, -1), vt)
            w_out = cd_wo
        ffn = functools.partial(_mix_ffn, h, ya.reshape(n, -1), yb.reshape(n, -1), w_out, j,
                                norm_ffn_g[layer], wg, wu, wd, layer)
        if layer == depth - 1:
            h = ffn(norm_final_g)
        else:
            h, u, q, k, vt = ffn(norm_mix_g[layer + 1], proj_of(layer + 1))
    return h.reshape(bsz, s, d)
```
